```python
import math
import jax
import jax.numpy as jnp
from jax import lax
import numpy as np

D_MODEL = 2048
BATCH = 16
SEQ = 2048
DEPTH = 2

N_BRANCH = 4
BRANCH_WIDTH = D_MODEL // 4
S5_GROUP = 16
S5_GROUPS = BRANCH_WIDTH // S5_GROUP
S5_STATE = 64
S5_MIN_NEG = 1e-4
HG_HEADS = 4
HG_DIM = BRANCH_WIDTH // HG_HEADS
HG_CHUNK = 64
HG_F_MIN = 1e-6
SB_HEADS = 4
SB_DIM = BRANCH_WIDTH // SB_HEADS
SB_BLOCK = 128
POOL_WINDOWS = (2, 4, 8, 16)
POOL_GROUP = BRANCH_WIDTH // len(POOL_WINDOWS)
IN_SLICES = 9
IN_COLS = IN_SLICES * BRANCH_WIDTH
PEER_HEADS = 8
PEER_NKEYS = 128
PEER_EXPERTS = PEER_NKEYS * PEER_NKEYS
PEER_TOPK = 16
PEER_QDIM = 256
PEER_CHUNK = 128
ADA_CHUNKS = 6
EPS = 1e-6

kernel_name = 'hybrid_s5_hgrn2_stickbreak_pool_peer'


def _rmsnorm(x, gain):
    xf = x.astype(jnp.float32)
    y = xf * lax.rsqrt(jnp.mean(xf * xf, axis=-1, keepdims=True) + EPS)
    return (y * gain.astype(jnp.float32)).astype(x.dtype)


def _modulate(h, shift, scale):
    return h * (1.0 + scale[:, None, :]) + shift[:, None, :]


def _split_heads(t, n_heads, head_dim):
    bsz, seq, _ = t.shape
    return t.astype(jnp.float32).reshape(bsz, seq, n_heads, head_dim).transpose(0, 2, 1, 3)


def _complex_affine_combine(e1, e2):
    a1r, a1i, b1r, b1i = e1
    a2r, a2i, b2r, b2i = e2
    return (a2r * a1r - a2i * a1i,
            a2r * a1i + a2i * a1r,
            a2r * b1r - a2i * b1i + b2r,
            a2r * b1i + a2i * b1r + b2i)


def _s5_mixer(u, lam_re, lam_im, log_step, b_re, b_im, c_re, c_im, d_skip, w_glu, b_glu):
    f32 = jnp.float32
    bsz, seq, _ = u.shape
    uf = u.astype(f32).reshape(bsz, seq, S5_GROUPS, S5_GROUP)
    lr = jnp.minimum(lam_re.astype(f32), -S5_MIN_NEG)
    li = lam_im.astype(f32)
    step = jnp.exp(log_step.astype(f32))[:, None]
    mag = jnp.exp(lr * step)
    ab_re = mag * jnp.cos(li * step)
    ab_im = mag * jnp.sin(li * step)
    den = lr * lr + li * li
    nr = ab_re - 1.0
    fr = (nr * lr + ab_im * li) / den
    fi = (ab_im * lr - nr * li) / den
    br, bi = b_re.astype(f32), b_im.astype(f32)
    bb_re = fr[..., None] * br - fi[..., None] * bi
    bb_im = fr[..., None] * bi + fi[..., None] * br
    bu_re = jnp.einsum('gph,btgh->btgp', bb_re, uf)
    bu_im = jnp.einsum('gph,btgh->btgp', bb_im, uf)
    a_re = jnp.broadcast_to(ab_re, bu_re.shape)
    a_im = jnp.broadcast_to(ab_im, bu_im.shape)
    _, _, xs_re, xs_im = lax.associative_scan(
        _complex_affine_combine, (a_re, a_im, bu_re, bu_im), axis=1)
    y = (jnp.einsum('ghp,btgp->btgh', c_re.astype(f32), xs_re)
         - jnp.einsum('ghp,btgp->btgh', c_im.astype(f32), xs_im)
         + d_skip.astype(f32).reshape(S5_GROUPS, S5_GROUP) * uf)
    y = jax.nn.gelu(y.reshape(bsz, seq, BRANCH_WIDTH), approximate=False)
    out = y * jax.nn.sigmoid(y @ w_glu.astype(f32) + b_glu.astype(f32))
    return out.astype(u.dtype)


def _hgrn2_mixer(q, f, i, g, lower_bound, norm_gain):
    f32 = jnp.float32
    bsz, seq, _ = q.shape
    lb = lower_bound.astype(f32).reshape(HG_HEADS, 1, HG_DIM)
    qh = jax.nn.silu(_split_heads(q, HG_HEADS, HG_DIM))
    vh = _split_heads(i, HG_HEADS, HG_DIM)
    f_gate = lb + (1.0 - lb) * jax.nn.sigmoid(_split_heads(f, HG_HEADS, HG_DIM))
    log_f = jnp.log(jnp.maximum(f_gate, HG_F_MIN))
    kh = 1.0 - f_gate
    n_chunks = seq // HG_CHUNK

    def to_chunks(t):
        return t.reshape(bsz, HG_HEADS, n_chunks, HG_CHUNK, HG_DIM).transpose(2, 0, 1, 3, 4)

    causal = jnp.tril(jnp.ones((HG_CHUNK, HG_CHUNK), dtype=bool))[:, :, None]

    def chunk_step(state, inp):
        qc, kc, vc, gc = inp
        bcum = jnp.cumsum(gc, axis=2)
        o_inter = jnp.einsum('bhtd,bhde->bhte', qc * jnp.exp(bcum), state)
        diff = bcum[:, :, :, None, :] - bcum[:, :, None, :, :]
        decay = jnp.where(causal, jnp.exp(jnp.where(causal, diff, 0.0)), 0.0)
        scores = jnp.einsum('bhtd,bhsd,bhtsd->bhts', qc, kc, decay)
        o_intra = jnp.einsum('bhts,bhse->bhte', scores, vc)
        b_last = bcum[:, :, -1:, :]
        k_dec = kc * jnp.exp(b_last - bcum)
        new_state = (jnp.exp(b_last[:, :, 0, :, None]) * state
                     + jnp.einsum('bhsd,bhse->bhde', k_dec, vc))
        return new_state, o_inter + o_intra

    state0 = jnp.zeros((bsz, HG_HEADS, HG_DIM, HG_DIM), f32)
    _, o = lax.scan(chunk_step, state0,
                    (to_chunks(qh), to_chunks(kh), to_chunks(vh), to_chunks(log_f)))
    o = o.transpose(1, 0, 3, 2, 4).reshape(bsz, seq, HG_HEADS, HG_DIM)
    o = (o * lax.rsqrt(jnp.mean(o * o, axis=-1, keepdims=True) + EPS)
         * norm_gain.astype(f32).reshape(HG_HEADS, HG_DIM))
    o = o.reshape(bsz, seq, BRANCH_WIDTH) * jax.nn.silu(g.astype(f32))
    return o.astype(q.dtype)


def _stick_breaking_mixer(q, k, v):
    bsz, seq, _ = q.shape
    qh = _split_heads(q, SB_HEADS, SB_DIM)
    kh = _split_heads(k, SB_HEADS, SB_DIM)
    vh = _split_heads(v, SB_HEADS, SB_DIM)
    scale = 1.0 / math.sqrt(SB_DIM)
    outs = []
    for blk in range(seq // SB_BLOCK):
        start = blk * SB_BLOCK
        end = start + SB_BLOCK
        z = jnp.einsum('bhtd,bhsd->bhts', qh[:, :, start:end], kh[:, :, :end]) * scale
        t_idx = start + jnp.arange(SB_BLOCK)[:, None]
        s_idx = jnp.arange(end)[None, :]
        mask = s_idx < t_idx
        log_not = jnp.where(mask, jax.nn.log_sigmoid(-z), 0.0)
        suffix = jnp.flip(jnp.cumsum(jnp.flip(log_not, -1), axis=-1), -1) - log_not
        weight = jnp.where(mask, jnp.exp(jax.nn.log_sigmoid(z) + suffix), 0.0)
        outs.append(jnp.einsum('bhts,bhse->bhte', weight, vh[:, :, :end]))
    o = jnp.concatenate(outs, axis=2)
    return o.transpose(0, 2, 1, 3).reshape(bsz, seq, BRANCH_WIDTH).astype(q.dtype)


def _pool_mixer(p, pool_w, pool_scale):
    f32 = jnp.float32
    bsz, seq, _ = p.shape
    pf = p.astype(f32).reshape(bsz, seq, len(POOL_WINDOWS), POOL_GROUP)
    csum = jnp.cumsum(pf, axis=1)
    pooled = []
    for gi, w in enumerate(POOL_WINDOWS):
        cs = csum[:, :, gi]
        lagged = jnp.pad(cs, ((0, 0), (w, 0), (0, 0)))[:, :seq]
        count = jnp.minimum(jnp.arange(1, seq + 1), w).astype(f32)[None, :, None]
        pooled.append((cs - lagged) / count - pf[:, :, gi])
    pooled = jnp.stack(pooled, axis=2)
    mixed = jnp.einsum('btgc,gcd->btgd', pooled, pool_w.astype(f32))
    return (mixed.reshape(bsz, seq, BRANCH_WIDTH) * pool_scale.astype(f32)).astype(p.dtype)


def _hybrid_mixer(h, w_in, s5_lambda_re, s5_lambda_im, s5_log_step, s5_b_re, s5_b_im,
                  s5_c_re, s5_c_im, s5_d, s5_w_glu, s5_b_glu, hg_lower_bound, hg_norm_gain,
                  pool_w, pool_scale, w_gate, w_branch, w_out):
    proj = h @ w_in
    s5_u, hg_q, hg_f, hg_i, hg_g, sb_q, sb_k, sb_v, pool_in = jnp.split(proj, IN_SLICES, axis=-1)
    branches = (
        _s5_mixer(s5_u, s5_lambda_re, s5_lambda_im, s5_log_step, s5_b_re, s5_b_im,
                  s5_c_re, s5_c_im, s5_d, s5_w_glu, s5_b_glu),
        _hgrn2_mixer(hg_q, hg_f, hg_i, hg_g, hg_lower_bound, hg_norm_gain),
        _stick_breaking_mixer(sb_q, sb_k, sb_v),
        _pool_mixer(pool_in, pool_w, pool_scale),
    )
    terms = [jax.nn.sigmoid(h @ w_gate[n]) * (yb @ w_branch[n]) for n, yb in enumerate(branches)]
    merged = terms[0] + terms[1] + terms[2] + terms[3]
    return merged @ w_out


def _peer(h, w_query, sub_keys, expert_u, expert_v):
    f32 = jnp.float32
    bsz, seq, d = h.shape
    tokens = h.reshape(-1, PEER_CHUNK, d)
    keys = sub_keys.astype(f32)

    def per_block(xc):
        q = (xc @ w_query).astype(f32).reshape(PEER_CHUNK, PEER_HEADS, 2, PEER_QDIM // 2)
        scores = jnp.einsum('thpd,hpkd->thpk', q, keys)
        top_val, top_idx = lax.top_k(scores, PEER_TOPK)
        cand = top_val[:, :, 0, :, None] + top_val[:, :, 1, None, :]
        cand_idx = top_idx[:, :, 0, :, None] * PEER_NKEYS + top_idx[:, :, 1, None, :]
        cand = cand.reshape(PEER_CHUNK, PEER_HEADS, PEER_TOPK * PEER_TOPK)
        cand_idx = cand_idx.reshape(PEER_CHUNK, PEER_HEADS, PEER_TOPK * PEER_TOPK)
        best_val, best_pos = lax.top_k(cand, PEER_TOPK)
        expert_idx = jnp.take_along_axis(cand_idx, best_pos, axis=-1)
        gate = jax.nn.softmax(best_val, axis=-1)
        u = jnp.take(expert_u, expert_idx, axis=0)
        v = jnp.take(expert_v, expert_idx, axis=0)
        act = jax.nn.gelu(jnp.einsum('td,thkd->thk', xc, u), approximate=False)
        return jnp.einsum('thk,thkd->td', (gate * act).astype(v.dtype), v)

    out = lax.map(per_block, tokens)
    return out.reshape(bsz, seq, d).astype(h.dtype)


def setup_inputs(seed: int = 0) -> dict:
    key = jax.random.key(seed)
    ks = jax.random.split(key, 32)
    f32 = jnp.float32

    def nrm(k, shape, scale):
        return jax.random.normal(k, shape, f32) * scale

    L, G, P, H = DEPTH, S5_GROUPS, S5_STATE, S5_GROUP
    return {
        'x': nrm(ks[0], (BATCH, SEQ, D_MODEL), 1.0),
        'c': nrm(ks[1], (BATCH, D_MODEL), 1.0),
        'ada_w': nrm(ks[2], (L, D_MODEL, ADA_CHUNKS * D_MODEL), 0.5 * D_MODEL ** -0.5),
        'ada_b': nrm(ks[3], (L, ADA_CHUNKS * D_MODEL), 0.02),
        'norm_mix_gain': 1.0 + nrm(ks[4], (L, D_MODEL), 0.02),
        'norm_ffn_gain': 1.0 + nrm(ks[5], (L, D_MODEL), 0.02),
        'w_in': nrm(ks[6], (L, D_MODEL, IN_COLS), D_MODEL ** -0.5),
        's5_lambda_re': -0.5 + nrm(ks[7], (L, G, P), 0.01),
        's5_lambda_im': math.pi * jnp.arange(P, dtype=f32) + nrm(ks[8], (L, G, P), 0.01),
        's5_log_step': jax.random.uniform(ks[9], (L, G), f32, math.log(1e-3), math.log(1e-1)),
        's5_b_re': nrm(ks[10], (L, G, P, H), (2.0 * H) ** -0.5),
        's5_b_im': nrm(ks[11], (L, G, P, H), (2.0 * H) ** -0.5),
        's5_c_re': nrm(ks[12], (L, G, H, P), P ** -0.5),
        's5_c_im': nrm(ks[13], (L, G, H, P), P ** -0.5),
        's5_d': nrm(ks[14], (L, BRANCH_WIDTH), 1.0),
        's5_w_glu': nrm(ks[15], (L, BRANCH_WIDTH, BRANCH_WIDTH), BRANCH_WIDTH ** -0.5),
        's5_b_glu': nrm(ks[16], (L, BRANCH_WIDTH), 0.02),
        'hg_lb_logits': 1.0 + nrm(ks[17], (L, BRANCH_WIDTH), 0.1),
        'hg_norm_gain': 1.0 + nrm(ks[18], (L, BRANCH_WIDTH), 0.02),
        'pool_w': nrm(ks[19], (L, len(POOL_WINDOWS), POOL_GROUP, POOL_GROUP), POOL_GROUP ** -0.5),
        'pool_scale': 1.0 + nrm(ks[20], (L, BRANCH_WIDTH), 0.02),
        'w_gate': nrm(ks[21], (L, N_BRANCH, D_MODEL, D_MODEL), D_MODEL ** -0.5),
        'w_branch': nrm(ks[22], (L, N_BRANCH, BRANCH_WIDTH, D_MODEL), BRANCH_WIDTH ** -0.5),
        'w_out': nrm(ks[23], (L, D_MODEL, D_MODEL), D_MODEL ** -0.5),
        'peer_w_query': nrm(ks[24], (L, D_MODEL, PEER_HEADS * PEER_QDIM), D_MODEL ** -0.5),
        'peer_sub_keys': nrm(ks[25], (L, PEER_HEADS, 2, PEER_NKEYS, PEER_QDIM // 2), (PEER_QDIM // 2) ** -0.5),
        'peer_u': nrm(ks[26], (L, PEER_EXPERTS, D_MODEL), D_MODEL ** -0.5),
        'peer_v': nrm(ks[27], (L, PEER_EXPERTS, D_MODEL), 0.5),
        'final_gain': 1.0 + nrm(ks[28], (D_MODEL,), 0.02),
    }


def reference(x, c, ada_w, ada_b, norm_mix_gain, norm_ffn_gain, w_in,
              s5_lambda_re, s5_lambda_im, s5_log_step, s5_b_re, s5_b_im, s5_c_re, s5_c_im,
              s5_d, s5_w_glu, s5_b_glu, hg_lb_logits, hg_norm_gain, pool_w, pool_scale,
              w_gate, w_branch, w_out, peer_w_query, peer_sub_keys, peer_u, peer_v, final_gain):
    lb_soft = jax.nn.softmax(hg_lb_logits.astype(jnp.float32), axis=0)
    lower_bounds = jnp.cumsum(lb_soft, axis=0) - lb_soft[0:1]
    cond = jax.nn.silu(c)
    for l in range(DEPTH):
        mod = cond @ ada_w[l] + ada_b[l]
        sh1, sc1, g1, sh2, sc2, g2 = jnp.split(mod, ADA_CHUNKS, axis=-1)
        h = _modulate(_rmsnorm(x, norm_mix_gain[l]), sh1, sc1)
        y = _hybrid_mixer(h, w_in[l], s5_lambda_re[l], s5_lambda_im[l], s5_log_step[l],
                          s5_b_re[l], s5_b_im[l], s5_c_re[l], s5_c_im[l], s5_d[l],
                          s5_w_glu[l], s5_b_glu[l], lower_bounds[l], hg_norm_gain[l],
                          pool_w[l], pool_scale[l], w_gate[l], w_branch[l], w_out[l])
        x = x + g1[:, None, :] * y
        h = _modulate(_rmsnorm(x, norm_ffn_gain[l]), sh2, sc2)
        x = x + g2[:, None, :] * _peer(h, peer_w_query[l], peer_sub_keys[l], peer_u[l], peer_v[l])
    return _rmsnorm(x, final_gain)
```

```python
import functools
import math

import jax
import jax.numpy as jnp
from jax import lax
from jax.experimental import pallas as pl
from jax.experimental.pallas import tpu as pltpu

F32 = jnp.float32
BF16 = jnp.bfloat16

D_MODEL = 2048
BRANCH_WIDTH = 512
N_BRANCH = 4
LANE = 128
S5_GROUP = 16
S5_GROUPS = 32
S5_STATE = 64
S5_MIN_NEG = 1e-4
S5_CHUNK = 16
HG_HEADS = 4
HG_F_MIN = 1e-6
HG_CHUNK = 128
HG_SUB = 16
SB_HEADS = 4
SB_DIM = 128
SB_BLOCK = 128
POOL_WINDOWS = (2, 4, 8, 16)
PEER_HEADS = 8
PEER_NKEYS = 128
PEER_TOPK = 16
EPS = 1e-6
NEG_BIG = -3.0e38

GATE_COLS = N_BRANCH * D_MODEL
IN_COLS = 9 * BRANCH_WIDTH
CAT_COLS = GATE_COLS + IN_COLS
PROJ_BLK0 = GATE_COLS // LANE
VMEM_LIMIT = 48 * 1024 * 1024


def _cparams(semantics, vmem=VMEM_LIMIT):
    return pltpu.CompilerParams(dimension_semantics=semantics, vmem_limit_bytes=vmem)


def _split3(x):
    hi = x.astype(BF16)
    r1 = x - hi.astype(F32)
    mid = r1.astype(BF16)
    lo = (r1 - mid.astype(F32)).astype(BF16)
    return hi, mid, lo


def _dot01_left(m01, x):
    hi, mid, lo = _split3(x)
    d = functools.partial(jnp.dot, preferred_element_type=F32)
    return d(m01, hi) + d(m01, mid) + d(m01, lo)


def _dot01_right(x, m01):
    hi, mid, lo = _split3(x)
    d = functools.partial(jnp.dot, preferred_element_type=F32)
    return d(hi, m01) + d(mid, m01) + d(lo, m01)


def _dot_f32(a, b):
    ah, am, al = _split3(a)
    bh, bm, bl = _split3(b)
    d = functools.partial(jnp.dot, preferred_element_type=F32)
    return (d(ah, bh) + (d(ah, bm) + d(am, bh))) + ((d(ah, bl) + d(al, bh)) + d(am, bm))


def _gelu(x):
    return 0.5 * x * (1.0 + lax.erf(x * (1.0 / math.sqrt(2.0))))


def _sigmoid(x):
    return 1.0 / (1.0 + jnp.exp(-x))


def _silu(x):
    return x * _sigmoid(x)


def _rms_modulate(x, gain, shift, scale):
    y = x * lax.rsqrt(jnp.mean(x * x, axis=-1, keepdims=True) + EPS) * gain
    return y * (1.0 + scale) + shift


def _ada_kernel(c_ref, w_ref, b_ref, o_ref):
    o_ref[...] = _dot_f32(_silu(c_ref[...]), w_ref[...]) + b_ref[...]


def _ada(c, w, b):
    bsz, d = c.shape
    n = w.shape[1]
    tn = 1024
    return pl.pallas_call(
        _ada_kernel,
        grid=(n // tn,),
        in_specs=[pl.BlockSpec((bsz, d), lambda j: (0, 0)),
                  pl.BlockSpec((d, tn), lambda j: (0, j)),
                  pl.BlockSpec((1, tn), lambda j: (0, j))],
        out_specs=pl.BlockSpec((bsz, tn), lambda j: (0, j)),
        out_shape=jax.ShapeDtypeStruct((bsz, n), F32),
        compiler_params=_cparams(("arbitrary",)),
        name="ada_mod",
    )(c, w, b.reshape(1, n))


def _inproj_kernel(x_ref, gain_ref, sh_ref, sc_ref, w_ref, o_ref, h_ref, *, gate_blocks):
    j = pl.program_id(1)

    @pl.when(j == 0)
    def _():
        h_ref[...] = _rms_modulate(x_ref[...], gain_ref[...], sh_ref[0], sc_ref[0]).astype(BF16)

    r = jnp.dot(h_ref[...], w_ref[...], preferred_element_type=F32)
    o_ref[...] = jnp.where(j < gate_blocks, _sigmoid(r), r).astype(o_ref.dtype)


def _inproj(x2d, gain, shift, scale, wcat, seq):
    t, d = x2d.shape
    n = wcat.shape[1]
    tm = min(1024, seq)
    tn = 512
    per_b = seq // tm
    return pl.pallas_call(
        functools.partial(_inproj_kernel, gate_blocks=GATE_COLS // tn),
        grid=(t // tm, n // tn),
        in_specs=[pl.BlockSpec((tm, d), lambda i, j: (i, 0)),
                  pl.BlockSpec((1, d), lambda i, j: (0, 0)),
                  pl.BlockSpec((1, 1, d), lambda i, j: (i // per_b, 0, 0)),
                  pl.BlockSpec((1, 1, d), lambda i, j: (i // per_b, 0, 0)),
                  pl.BlockSpec((d, tn), lambda i, j: (0, j))],
        out_specs=pl.BlockSpec((tm, tn), lambda i, j: (i, j)),
        out_shape=jax.ShapeDtypeStruct((t, n), BF16),
        scratch_shapes=[pltpu.VMEM((tm, d), BF16)],
        compiler_params=_cparams(("parallel", "arbitrary")),
        name="inproj",
    )(x2d, gain.reshape(1, d), shift, scale, wcat)


def _s5_operators(lam_re, lam_im, log_step, b_re, b_im, c_re, c_im, d_skip):
    n, g, p, h = S5_CHUNK, S5_GROUPS, S5_STATE, S5_GROUP
    hp = lax.Precision.HIGHEST
    lr = jnp.minimum(lam_re.astype(F32), -S5_MIN_NEG)
    li = lam_im.astype(F32)
    step = jnp.exp(log_step.astype(F32))[:, None]
    mag = jnp.exp(lr * step)
    ab_re = mag * jnp.cos(li * step)
    ab_im = mag * jnp.sin(li * step)
    den = lr * lr + li * li
    nr = ab_re - 1.0
    fr = (nr * lr + ab_im * li) / den
    fi = (ab_im * lr - nr * li) / den
    br, bi = b_re.astype(F32), b_im.astype(F32)
    bb_re = fr[..., None] * br - fi[..., None] * bi
    bb_im = fr[..., None] * bi + fi[..., None] * br
    prs, pis = [jnp.ones_like(ab_re)], [jnp.zeros_like(ab_re)]
    for _ in range(n):
        pr_, pi_ = prs[-1], pis[-1]
        prs.append(pr_ * ab_re - pi_ * ab_im)
        pis.append(pr_ * ab_im + pi_ * ab_re)
    pw_re, pw_im = jnp.stack(prs), jnp.stack(pis)
    cr, ci = c_re.astype(F32), c_im.astype(F32)
    ca_re = cr[None] * pw_re[:, :, None, :] - ci[None] * pw_im[:, :, None, :]
    ca_im = cr[None] * pw_im[:, :, None, :] + ci[None] * pw_re[:, :, None, :]
    kern = (jnp.einsum('kghp,gpj->kghj', ca_re[:n], bb_re, precision=hp)
            - jnp.einsum('kghp,gpj->kghj', ca_im[:n], bb_im, precision=hp))
    lag = jnp.arange(n)[None, :] - jnp.arange(n)[:, None]
    tm = jnp.where((lag >= 0)[:, :, None, None, None], kern[jnp.clip(lag, 0)], 0.0)
    tm = tm.transpose(2, 0, 4, 1, 3)
    eye_n = jnp.eye(n, dtype=F32)
    eye_h = jnp.eye(h, dtype=F32)
    dsk = d_skip.astype(F32).reshape(g, h)
    tm = tm + (eye_n[None, :, None, :, None] * eye_h[None, None, :, None, :]
               * dsk[:, None, None, None, :])
    tm = tm.reshape(g, n * h, n * h)
    rev = pw_re[n - 1 - jnp.arange(n)], pw_im[n - 1 - jnp.arange(n)]
    q_re = (rev[0][:, :, :, None] * bb_re[None] - rev[1][:, :, :, None] * bb_im[None])
    q_im = (rev[0][:, :, :, None] * bb_im[None] + rev[1][:, :, :, None] * bb_re[None])
    q_re = q_re.transpose(1, 0, 3, 2).reshape(g, n * h, p)
    q_im = q_im.transpose(1, 0, 3, 2).reshape(g, n * h, p)
    r_re = ca_re[1:].transpose(1, 3, 0, 2).reshape(g, p, n * h)
    r_im = (-ca_im[1:]).transpose(1, 3, 0, 2).reshape(g, p, n * h)

    def blockdiag(m):
        a, b = m.shape[1:]
        m2 = m.reshape(g // 2, 2, a, b)
        out = jnp.einsum('qiab,ij->qiajb', m2, jnp.eye(2, dtype=F32))
        return out.reshape(g // 2, 2 * a, 2 * b)

    w1 = jnp.concatenate([blockdiag(tm), blockdiag(q_re), blockdiag(q_im)], axis=2).astype(BF16)
    w2 = jnp.concatenate([blockdiag(r_re), blockdiag(r_im)], axis=1).astype(BF16)
    a16 = jnp.stack([pw_re[n].reshape(g // 2, 2 * p), pw_im[n].reshape(g // 2, 2 * p)], axis=1)
    return w1, w2, a16.astype(F32)


def _s5_kernel(u_ref, w1_ref, w2_ref, a_ref, y_ref, yl_scr, s_scr, x_scr, *, nb, nc):
    pw = 2 * S5_GROUP * S5_CHUNK
    ps = 2 * S5_STATE
    r = jnp.dot(u_ref[0], w1_ref[0], preferred_element_type=F32)
    yl_scr[...] = r[:, :pw]
    s_scr[...] = r[:, pw:]
    a_re = a_ref[0, 0:1, :]
    a_im = a_ref[0, 1:2, :]

    def step(c, carry):
        xr, xi = carry
        off = pl.multiple_of(c * nb, nb)
        x_scr[pl.ds(off, nb), :] = jnp.concatenate([xr, xi], axis=1)
        s = s_scr[pl.ds(off, nb), :]
        return (a_re * xr - a_im * xi + s[:, :ps], a_re * xi + a_im * xr + s[:, ps:])

    zero = jnp.zeros((nb, ps), F32)
    lax.fori_loop(0, nc, step, (zero, zero))
    y = yl_scr[...] + jnp.dot(x_scr[...].astype(BF16), w2_ref[0], preferred_element_type=F32)
    y_ref[0] = y.astype(y_ref.dtype)


def _s5_core(u, w1, w2, a16):
    bsz, seq, _ = u.shape
    n, h = S5_CHUNK, S5_GROUP
    nc = seq // n
    npair = S5_GROUPS // 2
    pw = 2 * h * n
    ps = 2 * S5_STATE
    rows = nc * bsz
    ur = u.reshape(bsz, nc, n, npair, 2, h).transpose(3, 1, 0, 4, 2, 5).reshape(npair, rows, pw)
    y = pl.pallas_call(
        functools.partial(_s5_kernel, nb=bsz, nc=nc),
        grid=(npair,),
        in_specs=[pl.BlockSpec((1, rows, pw), lambda q: (q, 0, 0)),
                  pl.BlockSpec((1, pw, pw + 2 * ps), lambda q: (q, 0, 0)),
                  pl.BlockSpec((1, 2 * ps, pw), lambda q: (q, 0, 0)),
                  pl.BlockSpec((1, 2, ps), lambda q: (q, 0, 0))],
        out_specs=pl.BlockSpec((1, rows, pw), lambda q: (q, 0, 0)),
        out_shape=jax.ShapeDtypeStruct((npair, rows, pw), BF16),
        scratch_shapes=[pltpu.VMEM((rows, pw), F32), pltpu.VMEM((rows, 2 * ps), F32),
                        pltpu.VMEM((rows, 2 * ps), F32)],
        compiler_params=_cparams(("parallel",)),
        name="s5_core",
    )(ur, w1, w2, a16)
    return y.reshape(npair, nc, bsz, 2, n, h).transpose(2, 1, 4, 0, 3, 5).reshape(bsz, seq, BRANCH_WIDTH)


def _hgrn_kernel(q_ref, f_ref, i_ref, g_ref, lb_ref, gain_ref, tril_ref, o_ref, st_scr, *, seq):
    c, sub = HG_CHUNK, HG_SUB
    lb = lb_ref[0]
    gain = gain_ref[0]
    tril = tril_ref[...]
    row = lax.broadcasted_iota(jnp.int32, (c, LANE), 0)
    rsub = jnp.bitwise_and(row, sub - 1)
    tcol = lax.broadcasted_iota(jnp.int32, (sub, c), 1)
    st_scr[...] = jnp.zeros_like(st_scr)
    nt = (((1,), (1,)), ((), ()))

    def chunk(ci, carry):
        off = pl.multiple_of(ci * c, c)
        q = _silu(q_ref[0, pl.ds(off, c), :].astype(F32))
        fg = lb + (1.0 - lb) * _sigmoid(f_ref[0, pl.ds(off, c), :].astype(F32))
        logf = jnp.log(jnp.maximum(fg, HG_F_MIN))
        k = 1.0 - fg
        v = i_ref[0, pl.ds(off, c), :]
        vf = v.astype(F32)
        bc = _dot01_left(tril, logf)
        o = jnp.zeros((c, LANE), F32)
        for lag in range(sub):
            if lag == 0:
                kd, bd, vd = k, bc, vf
            else:
                kd = pltpu.roll(k, lag, 0)
                bd = pltpu.roll(bc, lag, 0)
                vd = pltpu.roll(vf, lag, 0)
            valid = rsub >= lag
            e = jnp.exp(jnp.where(valid, bc - bd, 0.0))
            w = jnp.sum(q * kd * e, axis=-1, keepdims=True)
            o = o + jnp.where(valid, w * vd, 0.0)
        parts = []
        for j in range(c // sub - 1):
            lo, hi = sub * j, sub * (j + 1)
            bend = bc[hi - 1:hi, :]
            qj = (q * jnp.exp(jnp.minimum(bc - bend, 0.0))).astype(BF16)
            kj = (k[lo:hi] * jnp.exp(bend - bc[lo:hi])).astype(BF16)
            st = lax.dot_general(kj, qj, nt, preferred_element_type=F32)
            parts.append(jnp.where(tcol >= hi, st, 0.0))
        parts.append(jnp.zeros((sub, c), F32))
        scores = jnp.concatenate(parts, axis=0).T
        o = o + jnp.dot(scores.astype(BF16), v, preferred_element_type=F32)
        st_prev = st_scr[...]
        qe = (q * jnp.exp(bc)).astype(BF16)
        o = o + lax.dot_general(qe, st_prev.astype(BF16), nt, preferred_element_type=F32)
        blast = bc[c - 1:c, :]
        kdec = (k * jnp.exp(blast - bc)).astype(BF16)
        st_scr[...] = st_prev * jnp.exp(blast) + jnp.dot(vf.T.astype(BF16), kdec,
                                                         preferred_element_type=F32)
        o = o * lax.rsqrt(jnp.mean(o * o, axis=-1, keepdims=True) + EPS) * gain
        o = o * _silu(g_ref[0, pl.ds(off, c), :].astype(F32))
        o_ref[0, pl.ds(off, c), :] = o.astype(o_ref.dtype)
        return carry

    lax.fori_loop(0, seq // c, chunk, 0)


def _hgrn(proj3, lower_bound, norm_gain):
    bsz, seq, _ = proj3.shape
    c = HG_CHUNK
    tril = jnp.tril(jnp.ones((c, c), F32)).astype(BF16)
    base = PROJ_BLK0 + BRANCH_WIDTH // LANE

    def col(k):
        return pl.BlockSpec((1, seq, LANE), lambda b, h: (b, 0, base + k * HG_HEADS + h))

    vec = pl.BlockSpec((1, 1, LANE), lambda b, h: (h, 0, 0))
    return pl.pallas_call(
        functools.partial(_hgrn_kernel, seq=seq),
        grid=(bsz, HG_HEADS),
        in_specs=[col(0), col(1), col(2), col(3), vec, vec,
                  pl.BlockSpec((c, c), lambda b, h: (0, 0))],
        out_specs=pl.BlockSpec((1, seq, LANE), lambda b, h: (b, 0, h)),
        out_shape=jax.ShapeDtypeStruct((bsz, seq, BRANCH_WIDTH), BF16),
        scratch_shapes=[pltpu.VMEM((LANE, LANE), F32)],
        compiler_params=_cparams(("parallel", "parallel")),
        name="hgrn2",
    )(proj3, proj3, proj3, proj3, lower_bound.reshape(HG_HEADS, 1, LANE),
      norm_gain.reshape(HG_HEADS, 1, LANE), tril)


def _sb_kernel(q_ref, k_ref, v_ref, up_ref, o_ref):
    qi = pl.program_id(2)
    blk = SB_BLOCK
    q = q_ref[0]
    up = up_ref[...]
    scale = 1.0 / math.sqrt(SB_DIM)
    row = lax.broadcasted_iota(jnp.int32, (blk, blk), 0)
    colv = lax.broadcasted_iota(jnp.int32, (blk, blk), 1)
    below = colv < row
    nt = (((1,), (1,)), ((), ()))

    def body(jj, carry):
        acc, later = carry
        off = pl.multiple_of((qi - jj) * blk, blk)
        kj = k_ref[0, pl.ds(off, blk), :]
        vj = v_ref[0, pl.ds(off, blk), :]
        z = lax.dot_general(q, kj, nt, preferred_element_type=F32) * scale
        sp = jnp.maximum(z, 0.0) + jnp.log1p(jnp.exp(-jnp.abs(z)))
        mask = jnp.logical_or(jj > 0, below)
        log_not = jnp.where(mask, -sp, 0.0)
        suffix = _dot01_right(log_not, up)
        w = jnp.where(mask, jnp.exp(z - sp + suffix + later), 0.0)
        acc = acc + jnp.dot(w.astype(BF16), vj, preferred_element_type=F32)
        return acc, later + jnp.sum(log_not, axis=-1, keepdims=True)

    acc, _ = lax.fori_loop(0, qi + 1, body,
                           (jnp.zeros((blk, SB_DIM), F32), jnp.zeros((blk, 1), F32)))
    o_ref[0] = acc.astype(o_ref.dtype)


def _stick_breaking(proj3):
    bsz, seq, _ = proj3.shape
    blk = SB_BLOCK
    base = PROJ_BLK0 + 5 * BRANCH_WIDTH // LANE
    up = (jnp.arange(blk)[:, None] > jnp.arange(blk)[None, :]).astype(BF16)
    return pl.pallas_call(
        _sb_kernel,
        grid=(bsz, SB_HEADS, seq // blk),
        in_specs=[pl.BlockSpec((1, blk, LANE), lambda b, h, i: (b, i, base + h)),
                  pl.BlockSpec((1, seq, LANE), lambda b, h, i: (b, 0, base + SB_HEADS + h)),
                  pl.BlockSpec((1, seq, LANE), lambda b, h, i: (b, 0, base + 2 * SB_HEADS + h)),
                  pl.BlockSpec((blk, blk), lambda b, h, i: (0, 0))],
        out_specs=pl.BlockSpec((1, blk, LANE), lambda b, h, i: (b, i, h)),
        out_shape=jax.ShapeDtypeStruct((bsz, seq, BRANCH_WIDTH), BF16),
        compiler_params=_cparams(("parallel", "parallel", "arbitrary")),
        name="stick_breaking",
    )(proj3, proj3, proj3, up)


def _pool_kernel(p_ref, bc_ref, bp_ref, w_ref, sc_ref, o_ref, *, seq):
    g = pl.program_id(1)
    tile = LANE
    window = lax.shift_left(jnp.int32(2), g)
    row = lax.broadcasted_iota(jnp.int32, (tile, LANE), 0)
    band_cur = bc_ref[0]
    band_prev = bp_ref[0]
    wmix = w_ref[0]
    scale = sc_ref[0]

    def body(i, carry):
        off = pl.multiple_of(i * tile, tile)
        poff = pl.multiple_of(jnp.maximum(i - 1, 0) * tile, tile)
        cur = p_ref[0, pl.ds(off, tile), :]
        prev = p_ref[0, pl.ds(poff, tile), :]
        win = jnp.dot(band_cur, cur, preferred_element_type=F32)
        win = win + jnp.where(i > 0, jnp.dot(band_prev, prev, preferred_element_type=F32), 0.0)
        count = jnp.minimum(row + (off + 1), window).astype(F32)
        pooled = win / count - cur.astype(F32)
        mixed = jnp.dot(pooled.astype(BF16), wmix, preferred_element_type=F32) * scale
        o_ref[0, pl.ds(off, tile), :] = mixed.astype(o_ref.dtype)
        return carry

    lax.fori_loop(0, seq // tile, body, 0)


def _pool(proj3, pool_w, pool_scale):
    bsz, seq, _ = proj3.shape
    ng = len(POOL_WINDOWS)
    base = PROJ_BLK0 + 8 * BRANCH_WIDTH // LANE
    t = jnp.arange(LANE)
    lag = t[:, None] - t[None, :]
    wins = jnp.asarray(POOL_WINDOWS)[:, None, None]
    band_cur = ((lag[None] >= 0) & (lag[None] < wins)).astype(BF16)
    band_prev = (((lag[None] + LANE) >= 0) & ((lag[None] + LANE) < wins)).astype(BF16)
    mat = pl.BlockSpec((1, LANE, LANE), lambda b, g: (g, 0, 0))
    return pl.pallas_call(
        functools.partial(_pool_kernel, seq=seq),
        grid=(bsz, ng),
        in_specs=[pl.BlockSpec((1, seq, LANE), lambda b, g: (b, 0, base + g)),
                  mat, mat, mat,
                  pl.BlockSpec((1, 1, LANE), lambda b, g: (g, 0, 0))],
        out_specs=pl.BlockSpec((1, seq, LANE), lambda b, g: (b, 0, g)),
        out_shape=jax.ShapeDtypeStruct((bsz, seq, BRANCH_WIDTH), BF16),
        compiler_params=_cparams(("parallel", "parallel")),
        name="pool",
    )(proj3, band_cur, band_prev, pool_w.astype(BF16), pool_scale.astype(F32).reshape(ng, 1, LANE))


def _merge_kernel(x_ref, g1_ref, s5_ref, hg_ref, sb_ref, po_ref, gt0_ref, gt1_ref, gt2_ref, gt3_ref,
                  wglu_ref, bglu_ref, wbr_ref, wout_ref, o_ref):
    dot = functools.partial(jnp.dot, preferred_element_type=F32)
    y = _gelu(s5_ref[...].astype(F32))
    s5o = y * _sigmoid(dot(y.astype(BF16), wglu_ref[...]) + bglu_ref[...])
    branches = (s5o.astype(BF16), hg_ref[...], sb_ref[...], po_ref[...])
    gates = (gt0_ref, gt1_ref, gt2_ref, gt3_ref)
    merged = None
    for n in range(N_BRANCH):
        term = gates[n][...].astype(F32) * dot(branches[n], wbr_ref[n])
        merged = term if merged is None else merged + term
    mix = dot(merged.astype(BF16), wout_ref[...])
    o_ref[...] = x_ref[...] + g1_ref[0] * mix


def _merge(x2d, g1, s5y, hg, sb, po, proj, w_glu, b_glu, w_branch, w_out, seq):
    t, d = x2d.shape
    bw = BRANCH_WIDTH
    tm = min(256, seq)
    per_b = seq // tm
    tok = lambda w: pl.BlockSpec((tm, w), lambda i: (i, 0))
    gate = lambda n: pl.BlockSpec((tm, d), lambda i, n=n: (i, n))
    const = lambda shape: pl.BlockSpec(shape, lambda i: (0,) * len(shape))
    return pl.pallas_call(
        _merge_kernel,
        grid=(t // tm,),
        in_specs=[tok(d), pl.BlockSpec((1, 1, d), lambda i: (i // per_b, 0, 0)),
                  tok(bw), tok(bw), tok(bw), tok(bw),
                  gate(0), gate(1), gate(2), gate(3),
                  const((bw, bw)), const((1, bw)), const((N_BRANCH, bw, d)), const((d, d))],
        out_specs=tok(d),
        out_shape=jax.ShapeDtypeStruct((t, d), F32),
        compiler_params=_cparams(("parallel",)),
        name="merge",
    )(x2d, g1, s5y, hg, sb, po, proj, proj, proj, proj,
      w_glu.astype(BF16), b_glu.astype(F32).reshape(1, bw), w_branch.astype(BF16), w_out.astype(BF16))


def _top_values(s, n):
    vals = []
    work = s
    for r in range(n):
        m = jnp.max(work, axis=0, keepdims=True)
        vals.append(m)
        if r + 1 < n:
            work = jnp.where(work >= m, NEG_BIG, work)
    return vals


def _route_kernel(x_ref, gain_ref, sh_ref, sc_ref, wq_ref, keys_ref,
                  ht_ref, s1_ref, s2_ref, e1_ref, e2_ref, tau_ref):
    k = PEER_TOPK
    h = _rms_modulate(x_ref[...], gain_ref[...], sh_ref[0], sc_ref[0])
    ht = h.T.astype(BF16)
    ht_ref[...] = ht
    qt = jnp.dot(wq_ref[...], ht, preferred_element_type=F32)
    for hd in range(PEER_HEADS):
        sc, top = [], []
        for half in range(2):
            lo = (2 * hd + half) * PEER_NKEYS
            qs = qt[lo:lo + PEER_NKEYS, :].astype(BF16)
            s = jnp.dot(keys_ref[hd, half], qs, preferred_element_type=F32)
            sc.append(s)
            top.append(_top_values(s, k))
        v2 = jnp.concatenate(top[1], axis=0)
        cand = jnp.concatenate([top[0][a] + v2 for a in range(k)], axis=0)
        best = _top_values(cand, k)
        z = None
        for r in range(k):
            e = jnp.exp(best[r] - best[0])
            z = e if z is None else z + e
        s1_ref[hd] = sc[0]
        s2_ref[hd] = sc[1]
        e1_ref[hd] = jnp.exp(sc[0] - top[0][0]) / z
        e2_ref[hd] = jnp.exp(sc[1] - top[1][0])
        tau_ref[hd] = best[k - 1]


def _route(x2d, gain, shift, scale, wq_t, keys, seq):
    t, d = x2d.shape
    tm = min(256, seq)
    per_b = seq // tm
    nh, nk = PEER_HEADS, PEER_NKEYS
    big = pl.BlockSpec((nh, nk, tm), lambda i: (0, 0, i))
    big_shape = jax.ShapeDtypeStruct((nh, nk, t), F32)
    return pl.pallas_call(
        _route_kernel,
        grid=(t // tm,),
        in_specs=[pl.BlockSpec((tm, d), lambda i: (i, 0)),
                  pl.BlockSpec((1, d), lambda i: (0, 0)),
                  pl.BlockSpec((1, 1, d), lambda i: (i // per_b, 0, 0)),
                  pl.BlockSpec((1, 1, d), lambda i: (i // per_b, 0, 0)),
                  pl.BlockSpec(wq_t.shape, lambda i: (0, 0)),
                  pl.BlockSpec(keys.shape, lambda i: (0, 0, 0, 0))],
        out_specs=[pl.BlockSpec((d, tm), lambda i: (0, i)), big, big, big, big,
                   pl.BlockSpec((nh, 1, tm), lambda i: (0, 0, i))],
        out_shape=[jax.ShapeDtypeStruct((d, t), BF16), big_shape, big_shape, big_shape, big_shape,
                   jax.ShapeDtypeStruct((nh, 1, t), F32)],
        compiler_params=_cparams(("parallel",)),
        name="peer_route",
    )(x2d, gain.reshape(1, d), shift, scale, wq_t, keys)


def _peer_kernel(ht_ref, u_ref, vt_ref, s1_ref, s2_ref, e1_ref, e2_ref, tau_ref, x_ref, g2_ref,
                 o_ref, acc_ref, act_ref, p_ref, *, te, tm, lw):
    e = pl.program_id(1)
    nk = PEER_NKEYS
    rows_per_step = te // nk

    @pl.when(e == 0)
    def _():
        acc_ref[...] = jnp.zeros_like(acc_ref)

    act_ref[...] = jnp.dot(u_ref[...], ht_ref[...], preferred_element_type=F32)
    for ii in range(rows_per_step):
        i = e * rows_per_step + ii
        for lc in range(tm // lw):
            ls = slice(lc * lw, (lc + 1) * lw)
            gate = jnp.zeros((nk, lw), F32)
            for hd in range(PEER_HEADS):
                total = s1_ref[hd, pl.ds(i, 1), ls] + s2_ref[hd, :, ls]
                prob = e1_ref[hd, pl.ds(i, 1), ls] * e2_ref[hd, :, ls]
                gate = gate + jnp.where(total >= tau_ref[hd, :, ls], prob, 0.0)
            a = act_ref[ii * nk:(ii + 1) * nk, ls]
            p_ref[ii * nk:(ii + 1) * nk, ls] = (gate * _gelu(a)).astype(BF16)
    acc_ref[...] += jnp.dot(vt_ref[...], p_ref[...], preferred_element_type=F32)

    @pl.when(e == pl.num_programs(1) - 1)
    def _():
        o_ref[...] = x_ref[...] + g2_ref[0] * acc_ref[...].T


def _peer(ht, u_bf, vt_bf, s1, s2, e1, e2, tau, x2d, g2, seq):
    t, d = x2d.shape
    ne = u_bf.shape[0]
    tm = min(512, seq)
    te = 512
    lw = 256
    per_b = seq // tm
    nh, nk = PEER_HEADS, PEER_NKEYS
    big = pl.BlockSpec((nh, nk, tm), lambda i, e: (0, 0, i))
    return pl.pallas_call(
        functools.partial(_peer_kernel, te=te, tm=tm, lw=lw),
        grid=(t // tm, ne // te),
        in_specs=[pl.BlockSpec((d, tm), lambda i, e: (0, i)),
                  pl.BlockSpec((te, d), lambda i, e: (e, 0)),
                  pl.BlockSpec((d, te), lambda i, e: (0, e)),
                  big, big, big, big,
                  pl.BlockSpec((nh, 1, tm), lambda i, e: (0, 0, i)),
                  pl.BlockSpec((tm, d), lambda i, e: (i, 0)),
                  pl.BlockSpec((1, 1, d), lambda i, e: (i // per_b, 0, 0))],
        out_specs=pl.BlockSpec((tm, d), lambda i, e: (i, 0)),
        out_shape=jax.ShapeDtypeStruct((t, d), F32),
        scratch_shapes=[pltpu.VMEM((d, tm), F32), pltpu.VMEM((te, tm), F32), pltpu.VMEM((te, tm), BF16)],
        compiler_params=_cparams(("parallel", "arbitrary"), vmem=56 * 1024 * 1024),
        name="peer_experts",
    )(ht, u_bf, vt_bf, s1, s2, e1, e2, tau, x2d, g2)


def _final_kernel(x_ref, gain_ref, o_ref):
    x = x_ref[...]
    o_ref[...] = x * lax.rsqrt(jnp.mean(x * x, axis=-1, keepdims=True) + EPS) * gain_ref[...]


def _final_norm(x2d, gain):
    t, d = x2d.shape
    tm = 512
    return pl.pallas_call(
        _final_kernel,
        grid=(t // tm,),
        in_specs=[pl.BlockSpec((tm, d), lambda i: (i, 0)), pl.BlockSpec((1, d), lambda i: (0, 0))],
        out_specs=pl.BlockSpec((tm, d), lambda i: (i, 0)),
        out_shape=jax.ShapeDtypeStruct((t, d), F32),
        compiler_params=_cparams(("parallel",)),
        name="final_norm",
    )(x2d, gain.astype(F32).reshape(1, d))


def _layer(x2d, c, bsz, seq, ada_w, ada_b, norm_mix_gain, norm_ffn_gain, w_in, s5_params, s5_w_glu,
           s5_b_glu, lower_bound, hg_norm_gain, pool_w, pool_scale, w_gate, w_branch, w_out,
           peer_w_query, peer_sub_keys, peer_u, peer_v):
    d = D_MODEL
    mod = _ada(c.astype(F32), ada_w.astype(F32), ada_b.astype(F32))
    sh1, sc1, g1, sh2, sc2, g2 = [m.reshape(bsz, 1, d) for m in jnp.split(mod, 6, axis=-1)]
    wcat = jnp.concatenate([w_gate[n] for n in range(N_BRANCH)] + [w_in], axis=1).astype(BF16)
    proj = _inproj(x2d, norm_mix_gain.astype(F32), sh1, sc1, wcat, seq)
    proj3 = proj.reshape(bsz, seq, CAT_COLS)
    w1, w2, a16 = _s5_operators(*s5_params)
    s5y = _s5_core(proj3[:, :, GATE_COLS:GATE_COLS + BRANCH_WIDTH], w1, w2, a16)
    hg = _hgrn(proj3, lower_bound, hg_norm_gain.astype(F32))
    sb = _stick_breaking(proj3)
    po = _pool(proj3, pool_w, pool_scale)
    t = bsz * seq
    flat = lambda a: a.reshape(t, BRANCH_WIDTH)
    x1 = _merge(x2d, g1, flat(s5y), flat(hg), flat(sb), flat(po), proj, s5_w_glu, s5_b_glu,
                w_branch, w_out, seq)
    ht, s1, s2, e1, e2, tau = _route(x1, norm_ffn_gain.astype(F32), sh2, sc2,
                                     peer_w_query.T.astype(BF16), peer_sub_keys.astype(BF16), seq)
    return _peer(ht, peer_u.astype(BF16), peer_v.T.astype(BF16), s1, s2, e1, e2, tau, x1, g2, seq)


def kernel(x, c, ada_w, ada_b, norm_mix_gain, norm_ffn_gain, w_in, s5_lambda_re, s5_lambda_im, s5_log_step, s5_b_re, s5_b_im, s5_c_re, s5_c_im, s5_d, s5_w_glu, s5_b_glu, hg_lb_logits, hg_norm_gain, pool_w, pool_scale, w_gate, w_branch, w_out, peer_w_query, peer_sub_keys, peer_u, peer_v, final_gain):
    bsz, seq, d = x.shape
    depth = ada_w.shape[0]
    lb_soft = jax.nn.softmax(hg_lb_logits.astype(F32), axis=0)
    lower_bounds = jnp.cumsum(lb_soft, axis=0) - lb_soft[0:1]
    x2d = x.astype(F32).reshape(bsz * seq, d)
    for l in range(depth):
        s5_params = (s5_lambda_re[l], s5_lambda_im[l], s5_log_step[l], s5_b_re[l], s5_b_im[l],
                     s5_c_re[l], s5_c_im[l], s5_d[l])
        x2d = _layer(x2d, c, bsz, seq, ada_w[l], ada_b[l], norm_mix_gain[l], norm_ffn_gain[l], w_in[l],
                     s5_params, s5_w_glu[l], s5_b_glu[l], lower_bounds[l], hg_norm_gain[l], pool_w[l],
                     pool_scale[l], w_gate[l], w_branch[l], w_out[l], peer_w_query[l],
                     peer_sub_keys[l], peer_u[l], peer_v[l])
    return _final_norm(x2d, final_gain).reshape(bsz, seq, d).astype(x.dtype)
```

```python
import functools
import math

import jax
import jax.numpy as jnp
from jax import lax
from jax.experimental import pallas as pl
from jax.experimental.pallas import tpu as pltpu

F32 = jnp.float32
BF16 = jnp.bfloat16

D_MODEL = 2048
BRANCH_WIDTH = 512
N_BRANCH = 4
LANE = 128
S5_GROUP = 16
S5_GROUPS = 32
S5_STATE = 64
S5_MIN_NEG = 1e-4
S5_ROWS = 8
HG_HEADS = 4
HG_F_MIN = 1e-6
HG_CHUNK = 128
HG_SUB = 16
SB_HEADS = 4
SB_DIM = 128
SB_TILE = 256
POOL_WINDOWS = (2, 4, 8, 16)
PEER_HEADS = 8
PEER_NKEYS = 128
PEER_TOPK = 16
EPS = 1e-6
NEG_BIG = -3.0e38
NO_RANK = 255.0

GATE_COLS = N_BRANCH * D_MODEL
IN_COLS = 9 * BRANCH_WIDTH
CAT_COLS = GATE_COLS + IN_COLS
PROJ_BLK0 = GATE_COLS // LANE
VMEM_LIMIT = 48 * 1024 * 1024


def _cparams(semantics, vmem=VMEM_LIMIT):
    return pltpu.CompilerParams(dimension_semantics=semantics, vmem_limit_bytes=vmem)


def _split3(x):
    hi = x.astype(BF16)
    r1 = x - hi.astype(F32)
    mid = r1.astype(BF16)
    lo = (r1 - mid.astype(F32)).astype(BF16)
    return hi, mid, lo


def _dot01_left(m01, x):
    hi, mid, lo = _split3(x)
    d = functools.partial(jnp.dot, preferred_element_type=F32)
    return d(m01, hi) + d(m01, mid) + d(m01, lo)


def _dot01_right2(x, m01):
    hi = x.astype(BF16)
    lo = (x - hi.astype(F32)).astype(BF16)
    d = functools.partial(jnp.dot, preferred_element_type=F32)
    return d(hi, m01) + d(lo, m01)


def _dot_f32(a, b):
    ah, am, al = _split3(a)
    bh, bm, bl = _split3(b)
    d = functools.partial(jnp.dot, preferred_element_type=F32)
    return (d(ah, bh) + (d(ah, bm) + d(am, bh))) + ((d(ah, bl) + d(al, bh)) + d(am, bm))


def _gelu(x):
    return 0.5 * x * (1.0 + lax.erf(x * (1.0 / math.sqrt(2.0))))


def _sigmoid(x):
    return 0.5 * jnp.tanh(0.5 * x) + 0.5


def _silu(x):
    return x * _sigmoid(x)


def _rms_modulate(x, gain, shift, scale):
    y = x * lax.rsqrt(jnp.mean(x * x, axis=-1, keepdims=True) + EPS) * gain
    return y * (1.0 + scale) + shift


def _ada_kernel(c_ref, w_ref, b_ref, o_ref):
    o_ref[...] = _dot_f32(_silu(c_ref[...]), w_ref[...]) + b_ref[...]


def _ada(c, w, b):
    bsz, d = c.shape
    n = w.shape[1]
    tn = 1024
    return pl.pallas_call(
        _ada_kernel,
        grid=(n // tn,),
        in_specs=[pl.BlockSpec((bsz, d), lambda j: (0, 0)),
                  pl.BlockSpec((d, tn), lambda j: (0, j)),
                  pl.BlockSpec((1, tn), lambda j: (0, j))],
        out_specs=pl.BlockSpec((bsz, tn), lambda j: (0, j)),
        out_shape=jax.ShapeDtypeStruct((bsz, n), F32),
        compiler_params=_cparams(("arbitrary",)),
        name="ada_mod",
    )(c, w, b.reshape(1, n))


def _inproj_kernel(x_ref, gain_ref, sh_ref, sc_ref, w_ref, o_ref, h_ref, *, gate_blocks):
    j = pl.program_id(1)

    @pl.when(j == 0)
    def _():
        h_ref[...] = _rms_modulate(x_ref[...], gain_ref[...], sh_ref[0], sc_ref[0]).astype(BF16)

    r = jnp.dot(h_ref[...], w_ref[...], preferred_element_type=F32)

    @pl.when(j < gate_blocks)
    def _():
        o_ref[...] = _sigmoid(r).astype(o_ref.dtype)

    @pl.when(j >= gate_blocks)
    def _():
        o_ref[...] = r.astype(o_ref.dtype)


def _inproj(x2d, gain, shift, scale, wcat, seq):
    t, d = x2d.shape
    n = wcat.shape[1]
    tm = min(1024, seq)
    tn = 512
    per_b = seq // tm
    return pl.pallas_call(
        functools.partial(_inproj_kernel, gate_blocks=GATE_COLS // tn),
        grid=(t // tm, n // tn),
        in_specs=[pl.BlockSpec((tm, d), lambda i, j: (i, 0)),
                  pl.BlockSpec((1, d), lambda i, j: (0, 0)),
                  pl.BlockSpec((1, 1, d), lambda i, j: (i // per_b, 0, 0)),
                  pl.BlockSpec((1, 1, d), lambda i, j: (i // per_b, 0, 0)),
                  pl.BlockSpec((d, tn), lambda i, j: (0, j))],
        out_specs=pl.BlockSpec((tm, tn), lambda i, j: (i, j)),
        out_shape=jax.ShapeDtypeStruct((t, n), BF16),
        scratch_shapes=[pltpu.VMEM((tm, d), BF16)],
        compiler_params=_cparams(("parallel", "arbitrary")),
        name="inproj",
    )(x2d, gain.reshape(1, d), shift, scale, wcat)


def _s5_tables(lam_re, lam_im, log_step, b_re, b_im, c_re, c_im):
    g, p, h = S5_GROUPS, S5_STATE, S5_GROUP
    lr = jnp.minimum(lam_re.astype(F32), -S5_MIN_NEG)
    li = lam_im.astype(F32)
    step = jnp.exp(log_step.astype(F32))[:, None]
    mag = jnp.exp(lr * step)
    ab_re = mag * jnp.cos(li * step)
    ab_im = mag * jnp.sin(li * step)
    den = lr * lr + li * li
    nr = ab_re - 1.0
    fr = (nr * lr + ab_im * li) / den
    fi = (ab_im * lr - nr * li) / den
    br, bi = b_re.astype(F32), b_im.astype(F32)
    bb_re = fr[..., None] * br - fi[..., None] * bi
    bb_im = fr[..., None] * bi + fi[..., None] * br
    eye_g = jnp.eye(g, dtype=F32)

    def in_map(m):
        return jnp.einsum('gph,gk->ghkp', m, eye_g).reshape(g * h, g * p)

    def out_map(m):
        return jnp.einsum('ghp,gk->gpkh', m, eye_g).reshape(g * p, g * h)

    bbd = jnp.concatenate([in_map(bb_re), in_map(bb_im)], axis=1).astype(BF16)
    cbd = jnp.concatenate([out_map(c_re.astype(F32)), out_map(-c_im.astype(F32))], axis=0).astype(BF16)
    prs, pis = [ab_re], [ab_im]
    for _ in range(S5_ROWS - 1):
        pr_, pi_ = prs[-1], pis[-1]
        prs.append(pr_ * ab_re - pi_ * ab_im)
        pis.append(pr_ * ab_im + pi_ * ab_re)
    rows = jnp.arange(S5_ROWS)[:, None]
    tabs = []
    for d in (1, 2, 4):
        tabs.append(jnp.where(rows >= d, prs[d - 1].reshape(1, g * p), 0.0))
        tabs.append(jnp.where(rows >= d, pis[d - 1].reshape(1, g * p), 0.0))
    tabs.append(jnp.stack(prs).reshape(S5_ROWS, g * p))
    tabs.append(jnp.stack(pis).reshape(S5_ROWS, g * p))
    return bbd, cbd, jnp.stack(tabs).astype(F32)


def _s5_kernel(u_ref, bbd_ref, tab_ref, cbd_ref, d_ref, wglu_ref, bglu_ref, o_ref,
               bu_scr, x_scr, carry_scr, *, tile):
    ns = S5_GROUPS * S5_STATE
    lc = 512

    @pl.when(pl.program_id(1) == 0)
    def _():
        carry_scr[...] = jnp.zeros_like(carry_scr)

    u = u_ref[0]
    bu_scr[...] = jnp.dot(u, bbd_ref[...], preferred_element_type=F32)

    def block(r, c):
        off = pl.multiple_of(r * S5_ROWS, S5_ROWS)
        for ch in range(ns // lc):
            re_l = slice(ch * lc, (ch + 1) * lc)
            im_l = slice(ns + ch * lc, ns + (ch + 1) * lc)
            xr = bu_scr[pl.ds(off, S5_ROWS), re_l]
            xi = bu_scr[pl.ds(off, S5_ROWS), im_l]
            for k, d in enumerate((1, 2, 4)):
                sr = pltpu.roll(xr, d, 0)
                si = pltpu.roll(xi, d, 0)
                mr = tab_ref[2 * k, :, re_l]
                mi = tab_ref[2 * k + 1, :, re_l]
                xr, xi = xr + (mr * sr - mi * si), xi + (mr * si + mi * sr)
            ar = tab_ref[6, :, re_l]
            ai = tab_ref[7, :, re_l]
            cr = carry_scr[:, re_l]
            ci = carry_scr[:, im_l]
            xr, xi = xr + (ar * cr - ai * ci), xi + (ar * ci + ai * cr)
            x_scr[pl.ds(off, S5_ROWS), re_l] = xr
            x_scr[pl.ds(off, S5_ROWS), im_l] = xi
            carry_scr[:, re_l] = jnp.broadcast_to(xr[S5_ROWS - 1:S5_ROWS], (S5_ROWS, lc))
            carry_scr[:, im_l] = jnp.broadcast_to(xi[S5_ROWS - 1:S5_ROWS], (S5_ROWS, lc))
        return c

    lax.fori_loop(0, tile // S5_ROWS, block, 0)
    y = jnp.dot(x_scr[...].astype(BF16), cbd_ref[...], preferred_element_type=F32)
    y = _gelu(y + d_ref[...] * u.astype(F32))
    gate = _sigmoid(jnp.dot(y.astype(BF16), wglu_ref[...], preferred_element_type=F32) + bglu_ref[...])
    o_ref[0] = (y * gate).astype(o_ref.dtype)


def _s5(proj3, bbd, cbd, tabs, d_skip, w_glu, b_glu):
    bsz, seq, _ = proj3.shape
    bw = BRANCH_WIDTH
    ns = S5_GROUPS * S5_STATE
    tile = min(256, seq)
    const = lambda shape: pl.BlockSpec(shape, lambda b, i: (0,) * len(shape))
    return pl.pallas_call(
        functools.partial(_s5_kernel, tile=tile),
        grid=(bsz, seq // tile),
        in_specs=[pl.BlockSpec((1, tile, bw), lambda b, i: (b, i, GATE_COLS // bw)),
                  const((bw, 2 * ns)), const((8, S5_ROWS, ns)), const((2 * ns, bw)),
                  const((1, bw)), const((bw, bw)), const((1, bw))],
        out_specs=pl.BlockSpec((1, tile, bw), lambda b, i: (b, i, 0)),
        out_shape=jax.ShapeDtypeStruct((bsz, seq, bw), BF16),
        scratch_shapes=[pltpu.VMEM((tile, 2 * ns), F32), pltpu.VMEM((tile, 2 * ns), F32),
                        pltpu.VMEM((S5_ROWS, 2 * ns), F32)],
        compiler_params=_cparams(("parallel", "arbitrary")),
        name="s5",
    )(proj3, bbd, tabs, cbd, d_skip.astype(F32).reshape(1, bw), w_glu.astype(BF16),
      b_glu.astype(F32).reshape(1, bw))


def _hgrn_kernel(q_ref, f_ref, i_ref, g_ref, lb_ref, gain_ref, tril_ref, o_ref, st_scr, *, seq):
    c, sub = HG_CHUNK, HG_SUB
    lb = lb_ref[0]
    gain = gain_ref[0]
    tril = tril_ref[...]
    row = lax.broadcasted_iota(jnp.int32, (c, LANE), 0)
    rsub = jnp.bitwise_and(row, sub - 1)
    tcol = lax.broadcasted_iota(jnp.int32, (sub, c), 1)
    st_scr[...] = jnp.zeros_like(st_scr)
    nt = (((1,), (1,)), ((), ()))

    def chunk(ci, carry):
        off = pl.multiple_of(ci * c, c)
        q = _silu(q_ref[0, pl.ds(off, c), :].astype(F32))
        fg = lb + (1.0 - lb) * _sigmoid(f_ref[0, pl.ds(off, c), :].astype(F32))
        logf = jnp.log(jnp.maximum(fg, HG_F_MIN))
        k = 1.0 - fg
        v = i_ref[0, pl.ds(off, c), :]
        vf = v.astype(F32)
        bc = _dot01_left(tril, logf)
        o = jnp.zeros((c, LANE), F32)
        for lag in range(sub):
            if lag == 0:
                kd, bd, vd = k, bc, vf
            else:
                kd = pltpu.roll(k, lag, 0)
                bd = pltpu.roll(bc, lag, 0)
                vd = pltpu.roll(vf, lag, 0)
            valid = rsub >= lag
            e = jnp.exp(jnp.where(valid, bc - bd, 0.0))
            w = jnp.sum(q * kd * e, axis=-1, keepdims=True)
            o = o + jnp.where(valid, w * vd, 0.0)
        parts = []
        for j in range(c // sub - 1):
            lo, hi = sub * j, sub * (j + 1)
            bend = bc[hi - 1:hi, :]
            qj = (q * jnp.exp(jnp.minimum(bc - bend, 0.0))).astype(BF16)
            kj = (k[lo:hi] * jnp.exp(bend - bc[lo:hi])).astype(BF16)
            st = lax.dot_general(kj, qj, nt, preferred_element_type=F32)
            parts.append(jnp.where(tcol >= hi, st, 0.0))
        parts.append(jnp.zeros((sub, c), F32))
        scores = jnp.concatenate(parts, axis=0).T
        o = o + jnp.dot(scores.astype(BF16), v, preferred_element_type=F32)
        st_prev = st_scr[...]
        qe = (q * jnp.exp(bc)).astype(BF16)
        o = o + lax.dot_general(qe, st_prev.astype(BF16), nt, preferred_element_type=F32)
        blast = bc[c - 1:c, :]
        kdec = (k * jnp.exp(blast - bc)).astype(BF16)
        st_scr[...] = st_prev * jnp.exp(blast) + jnp.dot(vf.T.astype(BF16), kdec,
                                                         preferred_element_type=F32)
        o = o * lax.rsqrt(jnp.mean(o * o, axis=-1, keepdims=True) + EPS) * gain
        o = o * _silu(g_ref[0, pl.ds(off, c), :].astype(F32))
        o_ref[0, pl.ds(off, c), :] = o.astype(o_ref.dtype)
        return carry

    lax.fori_loop(0, seq // c, chunk, 0)


def _hgrn(proj3, lower_bound, norm_gain):
    bsz, seq, _ = proj3.shape
    c = HG_CHUNK
    tril = jnp.tril(jnp.ones((c, c), F32)).astype(BF16)
    base = PROJ_BLK0 + BRANCH_WIDTH // LANE

    def col(k):
        return pl.BlockSpec((1, seq, LANE), lambda b, h: (b, 0, base + k * HG_HEADS + h))

    vec = pl.BlockSpec((1, 1, LANE), lambda b, h: (h, 0, 0))
    return pl.pallas_call(
        functools.partial(_hgrn_kernel, seq=seq),
        grid=(bsz, HG_HEADS),
        in_specs=[col(0), col(1), col(2), col(3), vec, vec,
                  pl.BlockSpec((c, c), lambda b, h: (0, 0))],
        out_specs=pl.BlockSpec((1, seq, LANE), lambda b, h: (b, 0, h)),
        out_shape=jax.ShapeDtypeStruct((bsz, seq, BRANCH_WIDTH), BF16),
        scratch_shapes=[pltpu.VMEM((LANE, LANE), F32)],
        compiler_params=_cparams(("parallel", "parallel")),
        name="hgrn2",
    )(proj3, proj3, proj3, proj3, lower_bound.reshape(HG_HEADS, 1, LANE),
      norm_gain.reshape(HG_HEADS, 1, LANE), tril)


def _sb_kernel(q_ref, k_ref, v_ref, up_ref, o_ref, acc_scr, later_scr):
    qi = pl.program_id(1)
    t = SB_TILE
    up = up_ref[...]
    scale = 1.0 / math.sqrt(SB_DIM)
    row = lax.broadcasted_iota(jnp.int32, (t, t), 0)
    colv = lax.broadcasted_iota(jnp.int32, (t, t), 1)
    below = colv < row
    nt = (((1,), (1,)), ((), ()))
    acc_scr[...] = jnp.zeros_like(acc_scr)
    later_scr[...] = jnp.zeros_like(later_scr)

    def body(jj, carry):
        off = pl.multiple_of((qi - jj) * t, t)
        mask = jnp.logical_or(jj > 0, below)
        for h in range(SB_HEADS):
            hl = slice(h * SB_DIM, (h + 1) * SB_DIM)
            kj = k_ref[0, pl.ds(off, t), hl]
            vj = v_ref[0, pl.ds(off, t), hl]
            z = lax.dot_general(q_ref[0, :, hl], kj, nt, preferred_element_type=F32) * scale
            sp = jnp.maximum(z, 0.0) + jnp.log(1.0 + jnp.exp(-jnp.abs(z)))
            log_not = jnp.where(mask, -sp, 0.0)
            suffix = _dot01_right2(log_not, up)
            later = later_scr[h]
            w = jnp.where(mask, jnp.exp(z - sp + suffix + later), 0.0)
            acc_scr[h] += jnp.dot(w.astype(BF16), vj, preferred_element_type=F32)
            later_scr[h] = later + jnp.sum(log_not, axis=-1, keepdims=True)
        return carry

    lax.fori_loop(0, qi + 1, body, 0)
    for h in range(SB_HEADS):
        o_ref[0, :, h * SB_DIM:(h + 1) * SB_DIM] = acc_scr[h].astype(o_ref.dtype)


def _stick_breaking(proj3):
    bsz, seq, _ = proj3.shape
    t = SB_TILE
    bw = BRANCH_WIDTH
    base = (GATE_COLS + 5 * bw) // bw
    up = (jnp.arange(t)[:, None] > jnp.arange(t)[None, :]).astype(BF16)
    return pl.pallas_call(
        _sb_kernel,
        grid=(bsz, seq // t),
        in_specs=[pl.BlockSpec((1, t, bw), lambda b, i: (b, i, base)),
                  pl.BlockSpec((1, seq, bw), lambda b, i: (b, 0, base + 1)),
                  pl.BlockSpec((1, seq, bw), lambda b, i: (b, 0, base + 2)),
                  pl.BlockSpec((t, t), lambda b, i: (0, 0))],
        out_specs=pl.BlockSpec((1, t, bw), lambda b, i: (b, i, 0)),
        out_shape=jax.ShapeDtypeStruct((bsz, seq, bw), BF16),
        scratch_shapes=[pltpu.VMEM((SB_HEADS, t, SB_DIM), F32), pltpu.VMEM((SB_HEADS, t, 1), F32)],
        compiler_params=_cparams(("parallel", "arbitrary")),
        name="stick_breaking",
    )(proj3, proj3, proj3, up)


def _pool_kernel(p_ref, bc_ref, bp_ref, w_ref, sc_ref, o_ref, *, seq):
    g = pl.program_id(1)
    tile = LANE
    window = lax.shift_left(jnp.int32(2), g)
    row = lax.broadcasted_iota(jnp.int32, (tile, LANE), 0)
    band_cur = bc_ref[0]
    band_prev = bp_ref[0]
    wmix = w_ref[0]
    scale = sc_ref[0]

    def body(i, carry):
        off = pl.multiple_of(i * tile, tile)
        poff = pl.multiple_of(jnp.maximum(i - 1, 0) * tile, tile)
        cur = p_ref[0, pl.ds(off, tile), :]
        prev = p_ref[0, pl.ds(poff, tile), :]
        win = jnp.dot(band_cur, cur, preferred_element_type=F32)
        win = win + jnp.where(i > 0, jnp.dot(band_prev, prev, preferred_element_type=F32), 0.0)
        count = jnp.minimum(row + (off + 1), window).astype(F32)
        pooled = win / count - cur.astype(F32)
        mixed = jnp.dot(pooled.astype(BF16), wmix, preferred_element_type=F32) * scale
        o_ref[0, pl.ds(off, tile), :] = mixed.astype(o_ref.dtype)
        return carry

    lax.fori_loop(0, seq // tile, body, 0)


def _pool(proj3, pool_w, pool_scale):
    bsz, seq, _ = proj3.shape
    ng = len(POOL_WINDOWS)
    base = PROJ_BLK0 + 8 * BRANCH_WIDTH // LANE
    t = jnp.arange(LANE)
    lag = t[:, None] - t[None, :]
    wins = jnp.asarray(POOL_WINDOWS)[:, None, None]
    band_cur = ((lag[None] >= 0) & (lag[None] < wins)).astype(BF16)
    band_prev = (((lag[None] + LANE) >= 0) & ((lag[None] + LANE) < wins)).astype(BF16)
    mat = pl.BlockSpec((1, LANE, LANE), lambda b, g: (g, 0, 0))
    return pl.pallas_call(
        functools.partial(_pool_kernel, seq=seq),
        grid=(bsz, ng),
        in_specs=[pl.BlockSpec((1, seq, LANE), lambda b, g: (b, 0, base + g)),
                  mat, mat, mat,
                  pl.BlockSpec((1, 1, LANE), lambda b, g: (g, 0, 0))],
        out_specs=pl.BlockSpec((1, seq, LANE), lambda b, g: (b, 0, g)),
        out_shape=jax.ShapeDtypeStruct((bsz, seq, BRANCH_WIDTH), BF16),
        compiler_params=_cparams(("parallel", "parallel")),
        name="pool",
    )(proj3, band_cur, band_prev, pool_w.astype(BF16), pool_scale.astype(F32).reshape(ng, 1, LANE))


def _merge_kernel(x_ref, g1_ref, s5_ref, hg_ref, sb_ref, po_ref, gt0_ref, gt1_ref, gt2_ref, gt3_ref,
                  wbr_ref, wout_ref, o_ref):
    dot = functools.partial(jnp.dot, preferred_element_type=F32)
    branches = (s5_ref, hg_ref, sb_ref, po_ref)
    gates = (gt0_ref, gt1_ref, gt2_ref, gt3_ref)
    merged = None
    for n in range(N_BRANCH):
        term = gates[n][...].astype(F32) * dot(branches[n][...], wbr_ref[n])
        merged = term if merged is None else merged + term
    mix = dot(merged.astype(BF16), wout_ref[...])
    o_ref[...] = x_ref[...] + g1_ref[0] * mix


def _merge(x2d, g1, s5o, hg, sb, po, proj, w_branch, w_out, seq):
    t, d = x2d.shape
    bw = BRANCH_WIDTH
    tm = min(256, seq)
    per_b = seq // tm
    tok = lambda w: pl.BlockSpec((tm, w), lambda i: (i, 0))
    gate = lambda n: pl.BlockSpec((tm, d), lambda i, n=n: (i, n))
    const = lambda shape: pl.BlockSpec(shape, lambda i: (0,) * len(shape))
    return pl.pallas_call(
        _merge_kernel,
        grid=(t // tm,),
        in_specs=[tok(d), pl.BlockSpec((1, 1, d), lambda i: (i // per_b, 0, 0)),
                  tok(bw), tok(bw), tok(bw), tok(bw),
                  gate(0), gate(1), gate(2), gate(3),
                  const((N_BRANCH, bw, d)), const((d, d))],
        out_specs=tok(d),
        out_shape=jax.ShapeDtypeStruct((t, d), F32),
        compiler_params=_cparams(("parallel",)),
        name="merge",
    )(x2d, g1, s5o, hg, sb, po, proj, proj, proj, proj, w_branch.astype(BF16), w_out.astype(BF16))


def _top_values(s, n, with_rank=False):
    vals = []
    work = s
    rank = jnp.full(s.shape, NO_RANK, F32) if with_rank else None
    for r in range(n):
        m = jnp.max(work, axis=0, keepdims=True)
        vals.append(m)
        hit = work >= m
        if with_rank:
            rank = jnp.where(hit, float(r), rank)
        if with_rank or r + 1 < n:
            work = jnp.where(hit, NEG_BIG, work)
    return vals, rank


def _route_kernel(x_ref, gain_ref, sh_ref, sc_ref, wq_ref, keys_ref,
                  ht_ref, cnt_ref, e1_ref, rank_ref, e2_ref):
    k = PEER_TOPK
    h = _rms_modulate(x_ref[...], gain_ref[...], sh_ref[0], sc_ref[0])
    ht = h.T.astype(BF16)
    ht_ref[...] = ht
    qt = jnp.dot(wq_ref[...], ht, preferred_element_type=F32)
    for hd in range(PEER_HEADS):
        sc = []
        for half in range(2):
            lo = (2 * hd + half) * PEER_NKEYS
            qs = qt[lo:lo + PEER_NKEYS, :].astype(BF16)
            sc.append(jnp.dot(keys_ref[hd, half], qs, preferred_element_type=F32))
        top1, _ = _top_values(sc[0], k)
        top2, rank2 = _top_values(sc[1], k, with_rank=True)
        v1 = jnp.concatenate(top1, axis=0)
        v2 = jnp.concatenate(top2, axis=0)
        cand = [top1[0] + v2] + [top1[a] + v2[:8] for a in range(1, 8)] + [v1[8:] + top2[0]]
        best, _ = _top_values(jnp.concatenate(cand, axis=0), k)
        tau = best[k - 1]
        z = None
        for r in range(k):
            e = jnp.exp(best[r] - best[0])
            z = e if z is None else z + e
        count = jnp.zeros(sc[0].shape, F32)
        for b in range(k):
            count = count + jnp.where(sc[0] + top2[b] >= tau, 1.0, 0.0)
        cnt_ref[hd] = count
        e1_ref[hd] = jnp.exp(sc[0] - top1[0]) / z
        rank_ref[hd] = rank2.astype(BF16)
        e2_ref[hd] = jnp.exp(sc[1] - top2[0]).astype(BF16)


def _route(x2d, gain, shift, scale, wq_t, keys, seq):
    t, d = x2d.shape
    tm = min(256, seq)
    per_b = seq // tm
    nh, nk = PEER_HEADS, PEER_NKEYS
    big = pl.BlockSpec((nh, nk, tm), lambda i: (0, 0, i))
    big_f32 = jax.ShapeDtypeStruct((nh, nk, t), F32)
    big_bf16 = jax.ShapeDtypeStruct((nh, nk, t), BF16)
    return pl.pallas_call(
        _route_kernel,
        grid=(t // tm,),
        in_specs=[pl.BlockSpec((tm, d), lambda i: (i, 0)),
                  pl.BlockSpec((1, d), lambda i: (0, 0)),
                  pl.BlockSpec((1, 1, d), lambda i: (i // per_b, 0, 0)),
                  pl.BlockSpec((1, 1, d), lambda i: (i // per_b, 0, 0)),
                  pl.BlockSpec(wq_t.shape, lambda i: (0, 0)),
                  pl.BlockSpec(keys.shape, lambda i: (0, 0, 0, 0))],
        out_specs=[pl.BlockSpec((d, tm), lambda i: (0, i)), big, big, big, big],
        out_shape=[jax.ShapeDtypeStruct((d, t), BF16), big_f32, big_f32, big_bf16, big_bf16],
        compiler_params=_cparams(("parallel",)),
        name="peer_route",
    )(x2d, gain.reshape(1, d), shift, scale, wq_t, keys)


def _rows_bf16(row, n):
    tile = jnp.broadcast_to(row, (16, row.shape[1])).astype(BF16)
    return jnp.concatenate([tile] * (n // 16), axis=0)


def _peer_kernel(ht_ref, u_ref, vt_ref, cnt_ref, e1_ref, rank_ref, e2_ref, x_ref, g2_ref,
                 o_ref, acc_ref, p_ref, *, te, tm):
    e = pl.program_id(1)
    nk = PEER_NKEYS
    rows_per_step = te // nk

    @pl.when(e == 0)
    def _():
        acc_ref[...] = jnp.zeros_like(acc_ref)

    act = jnp.dot(u_ref[...], ht_ref[...], preferred_element_type=F32)
    for ii in range(rows_per_step):
        i = e * rows_per_step + ii
        gate = jnp.zeros((nk, tm), BF16)
        for hd in range(PEER_HEADS):
            count = _rows_bf16(cnt_ref[hd, pl.ds(i, 1), :], nk)
            e1 = _rows_bf16(e1_ref[hd, pl.ds(i, 1), :], nk)
            gate = gate + jnp.where(rank_ref[hd] < count, e1 * e2_ref[hd], jnp.zeros((), BF16))
        a = act[ii * nk:(ii + 1) * nk, :]
        p_ref[ii * nk:(ii + 1) * nk, :] = gate * _gelu(a).astype(BF16)
    acc_ref[...] += jnp.dot(vt_ref[...], p_ref[...], preferred_element_type=F32)

    @pl.when(e == pl.num_programs(1) - 1)
    def _():
        o_ref[...] = x_ref[...] + g2_ref[0] * acc_ref[...].T


def _peer(ht, u_bf, vt_bf, cnt, e1, rank2, e2, x2d, g2, seq):
    t, d = x2d.shape
    ne = u_bf.shape[0]
    tm = min(512, seq)
    te = 512
    per_b = seq // tm
    nh, nk = PEER_HEADS, PEER_NKEYS
    big = pl.BlockSpec((nh, nk, tm), lambda i, e: (0, 0, i))
    return pl.pallas_call(
        functools.partial(_peer_kernel, te=te, tm=tm),
        grid=(t // tm, ne // te),
        in_specs=[pl.BlockSpec((d, tm), lambda i, e: (0, i)),
                  pl.BlockSpec((te, d), lambda i, e: (e, 0)),
                  pl.BlockSpec((d, te), lambda i, e: (0, e)),
                  big, big, big, big,
                  pl.BlockSpec((tm, d), lambda i, e: (i, 0)),
                  pl.BlockSpec((1, 1, d), lambda i, e: (i // per_b, 0, 0))],
        out_specs=pl.BlockSpec((tm, d), lambda i, e: (i, 0)),
        out_shape=jax.ShapeDtypeStruct((t, d), F32),
        scratch_shapes=[pltpu.VMEM((d, tm), F32), pltpu.VMEM((te, tm), BF16)],
        compiler_params=_cparams(("parallel", "arbitrary")),
        name="peer_experts",
    )(ht, u_bf, vt_bf, cnt, e1, rank2, e2, x2d, g2)


def _final_kernel(x_ref, gain_ref, o_ref):
    x = x_ref[...]
    o_ref[...] = x * lax.rsqrt(jnp.mean(x * x, axis=-1, keepdims=True) + EPS) * gain_ref[...]


def _final_norm(x2d, gain):
    t, d = x2d.shape
    tm = 512
    return pl.pallas_call(
        _final_kernel,
        grid=(t // tm,),
        in_specs=[pl.BlockSpec((tm, d), lambda i: (i, 0)), pl.BlockSpec((1, d), lambda i: (0, 0))],
        out_specs=pl.BlockSpec((tm, d), lambda i: (i, 0)),
        out_shape=jax.ShapeDtypeStruct((t, d), F32),
        compiler_params=_cparams(("parallel",)),
        name="final_norm",
    )(x2d, gain.astype(F32).reshape(1, d))


def _layer(x2d, c, bsz, seq, ada_w, ada_b, norm_mix_gain, norm_ffn_gain, w_in, s5_params, s5_d,
           s5_w_glu, s5_b_glu, lower_bound, hg_norm_gain, pool_w, pool_scale, w_gate, w_branch, w_out,
           peer_w_query, peer_sub_keys, peer_u, peer_v):
    d = D_MODEL
    mod = _ada(c.astype(F32), ada_w.astype(F32), ada_b.astype(F32))
    sh1, sc1, g1, sh2, sc2, g2 = [m.reshape(bsz, 1, d) for m in jnp.split(mod, 6, axis=-1)]
    wcat = jnp.concatenate([w_gate[n] for n in range(N_BRANCH)] + [w_in], axis=1).astype(BF16)
    proj = _inproj(x2d, norm_mix_gain.astype(F32), sh1, sc1, wcat, seq)
    proj3 = proj.reshape(bsz, seq, CAT_COLS)
    bbd, cbd, tabs = _s5_tables(*s5_params)
    s5o = _s5(proj3, bbd, cbd, tabs, s5_d, s5_w_glu, s5_b_glu)
    hg = _hgrn(proj3, lower_bound, hg_norm_gain.astype(F32))
    sb = _stick_breaking(proj3)
    po = _pool(proj3, pool_w, pool_scale)
    t = bsz * seq
    flat = lambda a: a.reshape(t, BRANCH_WIDTH)
    x1 = _merge(x2d, g1, flat(s5o), flat(hg), flat(sb), flat(po), proj, w_branch, w_out, seq)
    ht, cnt, e1, rank2, e2 = _route(x1, norm_ffn_gain.astype(F32), sh2, sc2,
                                    peer_w_query.T.astype(BF16), peer_sub_keys.astype(BF16), seq)
    return _peer(ht, peer_u.astype(BF16), peer_v.T.astype(BF16), cnt, e1, rank2, e2, x1, g2, seq)


def kernel(x, c, ada_w, ada_b, norm_mix_gain, norm_ffn_gain, w_in, s5_lambda_re, s5_lambda_im, s5_log_step, s5_b_re, s5_b_im, s5_c_re, s5_c_im, s5_d, s5_w_glu, s5_b_glu, hg_lb_logits, hg_norm_gain, pool_w, pool_scale, w_gate, w_branch, w_out, peer_w_query, peer_sub_keys, peer_u, peer_v, final_gain):
    bsz, seq, d = x.shape
    depth = ada_w.shape[0]
    lb_soft = jax.nn.softmax(hg_lb_logits.astype(F32), axis=0)
    lower_bounds = jnp.cumsum(lb_soft, axis=0) - lb_soft[0:1]
    x2d = x.astype(F32).reshape(bsz * seq, d)
    for l in range(depth):
        s5_params = (s5_lambda_re[l], s5_lambda_im[l], s5_log_step[l], s5_b_re[l], s5_b_im[l],
                     s5_c_re[l], s5_c_im[l])
        x2d = _layer(x2d, c, bsz, seq, ada_w[l], ada_b[l], norm_mix_gain[l], norm_ffn_gain[l], w_in[l],
                     s5_params, s5_d[l], s5_w_glu[l], s5_b_glu[l], lower_bounds[l], hg_norm_gain[l],
                     pool_w[l], pool_scale[l], w_gate[l], w_branch[l], w_out[l], peer_w_query[l],
                     peer_sub_keys[l], peer_u[l], peer_v[l])
    return _final_norm(x2d, final_gain).reshape(bsz, seq, d).astype(x.dtype)
```

```python
import functools
import math

import jax
import jax.numpy as jnp
from jax import lax
from jax.experimental import pallas as pl
from jax.experimental.pallas import tpu as pltpu

F32 = jnp.float32
BF16 = jnp.bfloat16

D_MODEL = 2048
BRANCH_WIDTH = 512
N_BRANCH = 4
LANE = 128
S5_GROUP = 16
S5_GROUPS = 32
S5_STATE = 64
S5_MIN_NEG = 1e-4
S5_ROWS = 8
HG_HEADS = 4
HG_F_MIN = 1e-6
HG_CHUNK = 128
HG_SUB = 16
SB_HEADS = 4
SB_DIM = 128
SB_TILE = 256
POOL_WINDOWS = (2, 4, 8, 16)
PEER_HEADS = 8
PEER_NKEYS = 128
PEER_TOPK = 16
PEER_BLOCK = 256
EPS = 1e-6
NEG_BIG = -3.0e38
NO_RANK = 255.0

GATE_COLS = N_BRANCH * D_MODEL
IN_COLS = 9 * BRANCH_WIDTH
CAT_COLS = GATE_COLS + IN_COLS
PROJ_BLK0 = GATE_COLS // LANE
VMEM_LIMIT = 48 * 1024 * 1024


def _cparams(semantics, vmem=VMEM_LIMIT):
    return pltpu.CompilerParams(dimension_semantics=semantics, vmem_limit_bytes=vmem)


def _split3(x):
    hi = x.astype(BF16)
    r1 = x - hi.astype(F32)
    mid = r1.astype(BF16)
    lo = (r1 - mid.astype(F32)).astype(BF16)
    return hi, mid, lo


def _dot01_left(m01, x):
    hi, mid, lo = _split3(x)
    d = functools.partial(jnp.dot, preferred_element_type=F32)
    return d(m01, hi) + d(m01, mid) + d(m01, lo)


def _dot01_right2(x, m01):
    hi = x.astype(BF16)
    lo = (x - hi.astype(F32)).astype(BF16)
    d = functools.partial(jnp.dot, preferred_element_type=F32)
    return d(hi, m01) + d(lo, m01)


def _dot_f32(a, b):
    ah, am, al = _split3(a)
    bh, bm, bl = _split3(b)
    d = functools.partial(jnp.dot, preferred_element_type=F32)
    return (d(ah, bh) + (d(ah, bm) + d(am, bh))) + ((d(ah, bl) + d(al, bh)) + d(am, bm))


def _gelu(x):
    return 0.5 * x * (1.0 + lax.erf(x * (1.0 / math.sqrt(2.0))))


def _sigmoid(x):
    return 0.5 * jnp.tanh(0.5 * x) + 0.5


def _silu(x):
    return x * _sigmoid(x)


def _rms_modulate(x, gain, shift, scale):
    y = x * lax.rsqrt(jnp.mean(x * x, axis=-1, keepdims=True) + EPS) * gain
    return y * (1.0 + scale) + shift


def _ada_kernel(c_ref, w_ref, b_ref, o_ref):
    o_ref[...] = _dot_f32(_silu(c_ref[...]), w_ref[...]) + b_ref[...]


def _ada(c, w, b):
    bsz, d = c.shape
    n = w.shape[1]
    tn = 1024
    return pl.pallas_call(
        _ada_kernel,
        grid=(n // tn,),
        in_specs=[pl.BlockSpec((bsz, d), lambda j: (0, 0)),
                  pl.BlockSpec((d, tn), lambda j: (0, j)),
                  pl.BlockSpec((1, tn), lambda j: (0, j))],
        out_specs=pl.BlockSpec((bsz, tn), lambda j: (0, j)),
        out_shape=jax.ShapeDtypeStruct((bsz, n), F32),
        compiler_params=_cparams(("arbitrary",)),
        name="ada_mod",
    )(c, w, b.reshape(1, n))


def _inproj_kernel(x_ref, gain_ref, sh_ref, sc_ref, w_ref, o_ref, h_ref, *, gate_blocks):
    j = pl.program_id(1)

    @pl.when(j == 0)
    def _():
        def rows(r, c):
            sl = pl.ds(pl.multiple_of(r * LANE, LANE), LANE)
            h_ref[sl, :] = _rms_modulate(x_ref[sl, :], gain_ref[...], sh_ref[0], sc_ref[0]).astype(BF16)
            return c

        lax.fori_loop(0, x_ref.shape[0] // LANE, rows, 0)

    half = w_ref.shape[1] // 2
    parts = [jnp.dot(h_ref[...], w_ref[:, k * half:(k + 1) * half], preferred_element_type=F32)
             for k in range(2)]
    for k in range(2):
        r = parts[k]
        o_ref[:, k * half:(k + 1) * half] = jnp.where(j < gate_blocks, _sigmoid(r), r).astype(o_ref.dtype)


def _inproj(x2d, gain, shift, scale, wcat, seq):
    t, d = x2d.shape
    n = wcat.shape[1]
    tm = min(1024, seq)
    tn = 512
    per_b = seq // tm
    return pl.pallas_call(
        functools.partial(_inproj_kernel, gate_blocks=GATE_COLS // tn),
        grid=(t // tm, n // tn),
        in_specs=[pl.BlockSpec((tm, d), lambda i, j: (i, 0)),
                  pl.BlockSpec((1, d), lambda i, j: (0, 0)),
                  pl.BlockSpec((1, 1, d), lambda i, j: (i // per_b, 0, 0)),
                  pl.BlockSpec((1, 1, d), lambda i, j: (i // per_b, 0, 0)),
                  pl.BlockSpec((d, tn), lambda i, j: (0, j))],
        out_specs=pl.BlockSpec((tm, tn), lambda i, j: (i, j)),
        out_shape=jax.ShapeDtypeStruct((t, n), BF16),
        scratch_shapes=[pltpu.VMEM((tm, d), BF16)],
        compiler_params=_cparams(("parallel", "arbitrary")),
        name="inproj",
    )(x2d, gain.reshape(1, d), shift, scale, wcat)


def _s5_tables(lam_re, lam_im, log_step, b_re, b_im, c_re, c_im):
    g, p, h = S5_GROUPS, S5_STATE, S5_GROUP
    lr = jnp.minimum(lam_re.astype(F32), -S5_MIN_NEG)
    li = lam_im.astype(F32)
    step = jnp.exp(log_step.astype(F32))[:, None]
    mag = jnp.exp(lr * step)
    ab_re = mag * jnp.cos(li * step)
    ab_im = mag * jnp.sin(li * step)
    den = lr * lr + li * li
    nr = ab_re - 1.0
    fr = (nr * lr + ab_im * li) / den
    fi = (ab_im * lr - nr * li) / den
    br, bi = b_re.astype(F32), b_im.astype(F32)
    bb_re = fr[..., None] * br - fi[..., None] * bi
    bb_im = fr[..., None] * bi + fi[..., None] * br
    eye_g = jnp.eye(g, dtype=F32)

    def in_map(m):
        return jnp.einsum('gph,gk->ghkp', m, eye_g).reshape(g * h, g * p)

    def out_map(m):
        return jnp.einsum('ghp,gk->gpkh', m, eye_g).reshape(g * p, g * h)

    bbd = jnp.concatenate([in_map(bb_re), in_map(bb_im)], axis=1).astype(BF16)
    cbd = jnp.concatenate([out_map(c_re.astype(F32)), out_map(-c_im.astype(F32))], axis=0).astype(BF16)
    prs, pis = [ab_re], [ab_im]
    for _ in range(S5_ROWS - 1):
        pr_, pi_ = prs[-1], pis[-1]
        prs.append(pr_ * ab_re - pi_ * ab_im)
        pis.append(pr_ * ab_im + pi_ * ab_re)
    rows = jnp.arange(S5_ROWS)[:, None]
    tabs = []
    for d in (1, 2, 4):
        tabs.append(jnp.where(rows >= d, prs[d - 1].reshape(1, g * p), 0.0))
        tabs.append(jnp.where(rows >= d, pis[d - 1].reshape(1, g * p), 0.0))
    tabs.append(jnp.stack(prs).reshape(S5_ROWS, g * p))
    tabs.append(jnp.stack(pis).reshape(S5_ROWS, g * p))
    return bbd, cbd, jnp.stack(tabs).astype(F32)


def _s5_kernel(u_ref, bbd_ref, tab_ref, cbd_ref, d_ref, wglu_ref, bglu_ref, o_ref,
               bu_scr, x_scr, carry_scr, *, tile):
    ns = S5_GROUPS * S5_STATE
    lc = 512

    @pl.when(pl.program_id(1) == 0)
    def _():
        carry_scr[...] = jnp.zeros_like(carry_scr)

    u = u_ref[0]
    bu_scr[...] = jnp.dot(u, bbd_ref[...], preferred_element_type=F32)

    def block(r, c):
        off = pl.multiple_of(r * S5_ROWS, S5_ROWS)
        for ch in range(ns // lc):
            re_l = slice(ch * lc, (ch + 1) * lc)
            im_l = slice(ns + ch * lc, ns + (ch + 1) * lc)
            xr = bu_scr[pl.ds(off, S5_ROWS), re_l]
            xi = bu_scr[pl.ds(off, S5_ROWS), im_l]
            for k, d in enumerate((1, 2, 4)):
                sr = pltpu.roll(xr, d, 0)
                si = pltpu.roll(xi, d, 0)
                mr = tab_ref[2 * k, :, re_l]
                mi = tab_ref[2 * k + 1, :, re_l]
                xr, xi = xr + (mr * sr - mi * si), xi + (mr * si + mi * sr)
            ar = tab_ref[6, :, re_l]
            ai = tab_ref[7, :, re_l]
            cr = carry_scr[:, re_l]
            ci = carry_scr[:, im_l]
            xr, xi = xr + (ar * cr - ai * ci), xi + (ar * ci + ai * cr)
            x_scr[pl.ds(off, S5_ROWS), re_l] = xr
            x_scr[pl.ds(off, S5_ROWS), im_l] = xi
            carry_scr[:, re_l] = jnp.broadcast_to(xr[S5_ROWS - 1:S5_ROWS], (S5_ROWS, lc))
            carry_scr[:, im_l] = jnp.broadcast_to(xi[S5_ROWS - 1:S5_ROWS], (S5_ROWS, lc))
        return c

    lax.fori_loop(0, tile // S5_ROWS, block, 0)
    y = jnp.dot(x_scr[...].astype(BF16), cbd_ref[...], preferred_element_type=F32)
    y = _gelu(y + d_ref[...] * u.astype(F32))
    gate = _sigmoid(jnp.dot(y.astype(BF16), wglu_ref[...], preferred_element_type=F32) + bglu_ref[...])
    o_ref[0] = (y * gate).astype(o_ref.dtype)


def _s5(proj3, bbd, cbd, tabs, d_skip, w_glu, b_glu):
    bsz, seq, _ = proj3.shape
    bw = BRANCH_WIDTH
    ns = S5_GROUPS * S5_STATE
    tile = min(256, seq)
    const = lambda shape: pl.BlockSpec(shape, lambda b, i: (0,) * len(shape))
    return pl.pallas_call(
        functools.partial(_s5_kernel, tile=tile),
        grid=(bsz, seq // tile),
        in_specs=[pl.BlockSpec((1, tile, bw), lambda b, i: (b, i, GATE_COLS // bw)),
                  const((bw, 2 * ns)), const((8, S5_ROWS, ns)), const((2 * ns, bw)),
                  const((1, bw)), const((bw, bw)), const((1, bw))],
        out_specs=pl.BlockSpec((1, tile, bw), lambda b, i: (b, i, 0)),
        out_shape=jax.ShapeDtypeStruct((bsz, seq, bw), BF16),
        scratch_shapes=[pltpu.VMEM((tile, 2 * ns), F32), pltpu.VMEM((tile, 2 * ns), F32),
                        pltpu.VMEM((S5_ROWS, 2 * ns), F32)],
        compiler_params=_cparams(("parallel", "arbitrary")),
        name="s5",
    )(proj3, bbd, tabs, cbd, d_skip.astype(F32).reshape(1, bw), w_glu.astype(BF16),
      b_glu.astype(F32).reshape(1, bw))


def _hgrn_kernel(q_ref, f_ref, i_ref, g_ref, lb_ref, gain_ref, tril_ref, o_ref, st_scr, *, seq):
    c, sub = HG_CHUNK, HG_SUB
    lb = lb_ref[0]
    gain = gain_ref[0]
    tril = tril_ref[...]
    row = lax.broadcasted_iota(jnp.int32, (c, LANE), 0)
    rsub = jnp.bitwise_and(row, sub - 1)
    tcol = lax.broadcasted_iota(jnp.int32, (sub, c), 1)
    st_scr[...] = jnp.zeros_like(st_scr)
    nt = (((1,), (1,)), ((), ()))

    def chunk(ci, carry):
        off = pl.multiple_of(ci * c, c)
        q = _silu(q_ref[0, pl.ds(off, c), :].astype(F32))
        fg = lb + (1.0 - lb) * _sigmoid(f_ref[0, pl.ds(off, c), :].astype(F32))
        logf = jnp.log(jnp.maximum(fg, HG_F_MIN))
        k = 1.0 - fg
        v = i_ref[0, pl.ds(off, c), :]
        vf = v.astype(F32)
        bc = _dot01_left(tril, logf)
        o = jnp.zeros((c, LANE), F32)
        for lag in range(sub):
            if lag == 0:
                kd, bd, vd = k, bc, vf
            else:
                kd = pltpu.roll(k, lag, 0)
                bd = pltpu.roll(bc, lag, 0)
                vd = pltpu.roll(vf, lag, 0)
            valid = rsub >= lag
            e = jnp.exp(jnp.where(valid, bc - bd, 0.0))
            w = jnp.sum(q * kd * e, axis=-1, keepdims=True)
            o = o + jnp.where(valid, w * vd, 0.0)
        parts = []
        for j in range(c // sub - 1):
            lo, hi = sub * j, sub * (j + 1)
            bend = bc[hi - 1:hi, :]
            qj = (q * jnp.exp(jnp.minimum(bc - bend, 0.0))).astype(BF16)
            kj = (k[lo:hi] * jnp.exp(bend - bc[lo:hi])).astype(BF16)
            st = lax.dot_general(kj, qj, nt, preferred_element_type=F32)
            parts.append(jnp.where(tcol >= hi, st, 0.0))
        parts.append(jnp.zeros((sub, c), F32))
        scores = jnp.concatenate(parts, axis=0).T
        o = o + jnp.dot(scores.astype(BF16), v, preferred_element_type=F32)
        st_prev = st_scr[...]
        qe = (q * jnp.exp(bc)).astype(BF16)
        o = o + lax.dot_general(qe, st_prev.astype(BF16), nt, preferred_element_type=F32)
        blast = bc[c - 1:c, :]
        kdec = (k * jnp.exp(blast - bc)).astype(BF16)
        st_scr[...] = st_prev * jnp.exp(blast) + jnp.dot(vf.T.astype(BF16), kdec,
                                                         preferred_element_type=F32)
        o = o * lax.rsqrt(jnp.mean(o * o, axis=-1, keepdims=True) + EPS) * gain
        o = o * _silu(g_ref[0, pl.ds(off, c), :].astype(F32))
        o_ref[0, pl.ds(off, c), :] = o.astype(o_ref.dtype)
        return carry

    lax.fori_loop(0, seq // c, chunk, 0)


def _hgrn(proj3, lower_bound, norm_gain):
    bsz, seq, _ = proj3.shape
    c = HG_CHUNK
    tril = jnp.tril(jnp.ones((c, c), F32)).astype(BF16)
    base = PROJ_BLK0 + BRANCH_WIDTH // LANE

    def col(k):
        return pl.BlockSpec((1, seq, LANE), lambda b, h: (b, 0, base + k * HG_HEADS + h))

    vec = pl.BlockSpec((1, 1, LANE), lambda b, h: (h, 0, 0))
    return pl.pallas_call(
        functools.partial(_hgrn_kernel, seq=seq),
        grid=(bsz, HG_HEADS),
        in_specs=[col(0), col(1), col(2), col(3), vec, vec,
                  pl.BlockSpec((c, c), lambda b, h: (0, 0))],
        out_specs=pl.BlockSpec((1, seq, LANE), lambda b, h: (b, 0, h)),
        out_shape=jax.ShapeDtypeStruct((bsz, seq, BRANCH_WIDTH), BF16),
        scratch_shapes=[pltpu.VMEM((LANE, LANE), F32)],
        compiler_params=_cparams(("parallel", "parallel")),
        name="hgrn2",
    )(proj3, proj3, proj3, proj3, lower_bound.reshape(HG_HEADS, 1, LANE),
      norm_gain.reshape(HG_HEADS, 1, LANE), tril)


def _sb_kernel(q_ref, k_ref, v_ref, up_ref, o_ref, acc_scr, later_scr):
    qi = pl.program_id(1)
    t = SB_TILE
    up = up_ref[...]
    scale = 1.0 / math.sqrt(SB_DIM)
    row = lax.broadcasted_iota(jnp.int32, (t, t), 0)
    colv = lax.broadcasted_iota(jnp.int32, (t, t), 1)
    below = colv < row
    nt = (((1,), (1,)), ((), ()))
    acc_scr[...] = jnp.zeros_like(acc_scr)
    later_scr[...] = jnp.zeros_like(later_scr)

    def body(jj, carry):
        off = pl.multiple_of((qi - jj) * t, t)
        mask = jnp.logical_or(jj > 0, below)
        heads = [slice(h * SB_DIM, (h + 1) * SB_DIM) for h in range(SB_HEADS)]
        zs = [lax.dot_general(q_ref[0, :, hl], k_ref[0, pl.ds(off, t), hl], nt,
                              preferred_element_type=F32) * scale for hl in heads]
        logits, suffixes, totals = [], [], []
        for h in range(SB_HEADS):
            z = zs[h]
            sp = jnp.maximum(z, 0.0) + jnp.log(1.0 + jnp.exp(-jnp.abs(z)))
            log_not = jnp.where(mask, -sp, 0.0)
            suffixes.append(_dot01_right2(log_not, up))
            logits.append(z - sp)
            totals.append(jnp.sum(log_not, axis=-1, keepdims=True))
        for h in range(SB_HEADS):
            later = later_scr[h]
            w = jnp.where(mask, jnp.exp(logits[h] + suffixes[h] + later), 0.0)
            acc_scr[h] += jnp.dot(w.astype(BF16), v_ref[0, pl.ds(off, t), heads[h]],
                                  preferred_element_type=F32)
            later_scr[h] = later + totals[h]
        return carry

    lax.fori_loop(0, qi + 1, body, 0)
    for h in range(SB_HEADS):
        o_ref[0, :, h * SB_DIM:(h + 1) * SB_DIM] = acc_scr[h].astype(o_ref.dtype)


def _stick_breaking(proj3):
    bsz, seq, _ = proj3.shape
    t = SB_TILE
    bw = BRANCH_WIDTH
    base = (GATE_COLS + 5 * bw) // bw
    up = (jnp.arange(t)[:, None] > jnp.arange(t)[None, :]).astype(BF16)
    return pl.pallas_call(
        _sb_kernel,
        grid=(bsz, seq // t),
        in_specs=[pl.BlockSpec((1, t, bw), lambda b, i: (b, i, base)),
                  pl.BlockSpec((1, seq, bw), lambda b, i: (b, 0, base + 1)),
                  pl.BlockSpec((1, seq, bw), lambda b, i: (b, 0, base + 2)),
                  pl.BlockSpec((t, t), lambda b, i: (0, 0))],
        out_specs=pl.BlockSpec((1, t, bw), lambda b, i: (b, i, 0)),
        out_shape=jax.ShapeDtypeStruct((bsz, seq, bw), BF16),
        scratch_shapes=[pltpu.VMEM((SB_HEADS, t, SB_DIM), F32), pltpu.VMEM((SB_HEADS, t, 1), F32)],
        compiler_params=_cparams(("parallel", "arbitrary")),
        name="stick_breaking",
    )(proj3, proj3, proj3, up)


def _pool_kernel(p_ref, bc_ref, bp_ref, w_ref, sc_ref, o_ref, *, seq):
    g = pl.program_id(1)
    tile = LANE
    window = lax.shift_left(jnp.int32(2), g)
    row = lax.broadcasted_iota(jnp.int32, (tile, LANE), 0)
    band_cur = bc_ref[0]
    band_prev = bp_ref[0]
    wmix = w_ref[0]
    scale = sc_ref[0]

    def body(i, carry):
        off = pl.multiple_of(i * tile, tile)
        poff = pl.multiple_of(jnp.maximum(i - 1, 0) * tile, tile)
        cur = p_ref[0, pl.ds(off, tile), :]
        prev = p_ref[0, pl.ds(poff, tile), :]
        win = jnp.dot(band_cur, cur, preferred_element_type=F32)
        win = win + jnp.where(i > 0, jnp.dot(band_prev, prev, preferred_element_type=F32), 0.0)
        count = jnp.minimum(row + (off + 1), window).astype(F32)
        pooled = win / count - cur.astype(F32)
        mixed = jnp.dot(pooled.astype(BF16), wmix, preferred_element_type=F32) * scale
        o_ref[0, pl.ds(off, tile), :] = mixed.astype(o_ref.dtype)
        return carry

    lax.fori_loop(0, seq // tile, body, 0)


def _pool(proj3, pool_w, pool_scale):
    bsz, seq, _ = proj3.shape
    ng = len(POOL_WINDOWS)
    base = PROJ_BLK0 + 8 * BRANCH_WIDTH // LANE
    t = jnp.arange(LANE)
    lag = t[:, None] - t[None, :]
    wins = jnp.asarray(POOL_WINDOWS)[:, None, None]
    band_cur = ((lag[None] >= 0) & (lag[None] < wins)).astype(BF16)
    band_prev = (((lag[None] + LANE) >= 0) & ((lag[None] + LANE) < wins)).astype(BF16)
    mat = pl.BlockSpec((1, LANE, LANE), lambda b, g: (g, 0, 0))
    return pl.pallas_call(
        functools.partial(_pool_kernel, seq=seq),
        grid=(bsz, ng),
        in_specs=[pl.BlockSpec((1, seq, LANE), lambda b, g: (b, 0, base + g)),
                  mat, mat, mat,
                  pl.BlockSpec((1, 1, LANE), lambda b, g: (g, 0, 0))],
        out_specs=pl.BlockSpec((1, seq, LANE), lambda b, g: (b, 0, g)),
        out_shape=jax.ShapeDtypeStruct((bsz, seq, BRANCH_WIDTH), BF16),
        compiler_params=_cparams(("parallel", "parallel")),
        name="pool",
    )(proj3, band_cur, band_prev, pool_w.astype(BF16), pool_scale.astype(F32).reshape(ng, 1, LANE))


def _merge_kernel(x_ref, g1_ref, s5_ref, hg_ref, sb_ref, po_ref, gt0_ref, gt1_ref, gt2_ref, gt3_ref,
                  wbr_ref, wout_ref, o_ref):
    dot = functools.partial(jnp.dot, preferred_element_type=F32)
    branches = (s5_ref, hg_ref, sb_ref, po_ref)
    gates = (gt0_ref, gt1_ref, gt2_ref, gt3_ref)
    merged = None
    for n in range(N_BRANCH):
        term = gates[n][...].astype(F32) * dot(branches[n][...], wbr_ref[n])
        merged = term if merged is None else merged + term
    mix = dot(merged.astype(BF16), wout_ref[...])
    o_ref[...] = x_ref[...] + g1_ref[0] * mix


def _merge(x2d, g1, s5o, hg, sb, po, proj, w_branch, w_out, seq):
    t, d = x2d.shape
    bw = BRANCH_WIDTH
    tm = min(256, seq)
    per_b = seq // tm
    tok = lambda w: pl.BlockSpec((tm, w), lambda i: (i, 0))
    gate = lambda n: pl.BlockSpec((tm, d), lambda i, n=n: (i, n))
    const = lambda shape: pl.BlockSpec(shape, lambda i: (0,) * len(shape))
    return pl.pallas_call(
        _merge_kernel,
        grid=(t // tm,),
        in_specs=[tok(d), pl.BlockSpec((1, 1, d), lambda i: (i // per_b, 0, 0)),
                  tok(bw), tok(bw), tok(bw), tok(bw),
                  gate(0), gate(1), gate(2), gate(3),
                  const((N_BRANCH, bw, d)), const((d, d))],
        out_specs=tok(d),
        out_shape=jax.ShapeDtypeStruct((t, d), F32),
        compiler_params=_cparams(("parallel",)),
        name="merge",
    )(x2d, g1, s5o, hg, sb, po, proj, proj, proj, proj, w_branch.astype(BF16), w_out.astype(BF16))


def _top_values(s, n, with_rank=False):
    vals = []
    work = s
    rank = jnp.full(s.shape, NO_RANK, F32) if with_rank else None
    for r in range(n):
        m = jnp.max(work, axis=0, keepdims=True)
        vals.append(m)
        hit = work >= m
        if with_rank:
            rank = jnp.where(hit, float(r), rank)
        if with_rank or r + 1 < n:
            work = jnp.where(hit, NEG_BIG, work)
    return vals, rank


def _route_kernel(x_ref, gain_ref, sh_ref, sc_ref, wq_ref, keys_ref,
                  ht_ref, cnt_ref, e1_ref, rank_ref, e2_ref):
    k = PEER_TOPK
    h = _rms_modulate(x_ref[...], gain_ref[...], sh_ref[0], sc_ref[0])
    ht = h.T.astype(BF16)
    ht_ref[...] = ht
    qt = jnp.dot(wq_ref[...], ht, preferred_element_type=F32)
    for hd in range(PEER_HEADS):
        sc = []
        for half in range(2):
            lo = (2 * hd + half) * PEER_NKEYS
            qs = qt[lo:lo + PEER_NKEYS, :].astype(BF16)
            sc.append(jnp.dot(keys_ref[hd, half], qs, preferred_element_type=F32))
        top1, _ = _top_values(sc[0], k)
        top2, rank2 = _top_values(sc[1], k, with_rank=True)
        v1 = jnp.concatenate(top1, axis=0)
        v2 = jnp.concatenate(top2, axis=0)
        cand = [top1[0] + v2] + [top1[a] + v2[:8] for a in range(1, 8)] + [v1[8:] + top2[0]]
        best, _ = _top_values(jnp.concatenate(cand, axis=0), k)
        tau = best[k - 1]
        z = None
        for r in range(k):
            e = jnp.exp(best[r] - best[0])
            z = e if z is None else z + e
        count = jnp.zeros(sc[0].shape, F32)
        for b in range(k):
            count = count + jnp.where(sc[0] + top2[b] >= tau, 1.0, 0.0)
        cnt_ref[hd] = count
        e1_ref[hd] = jnp.exp(sc[0] - top1[0]) / z
        rank_ref[hd] = rank2.astype(BF16)
        e2_ref[hd] = jnp.exp(sc[1] - top2[0]).astype(BF16)


def _route(x2d, gain, shift, scale, wq_t, keys, seq):
    t, d = x2d.shape
    tm = min(256, seq)
    per_b = seq // tm
    nh, nk = PEER_HEADS, PEER_NKEYS
    big = pl.BlockSpec((nh, nk, tm), lambda i: (0, 0, i))
    big_f32 = jax.ShapeDtypeStruct((nh, nk, t), F32)
    big_bf16 = jax.ShapeDtypeStruct((nh, nk, t), BF16)
    return pl.pallas_call(
        _route_kernel,
        grid=(t // tm,),
        in_specs=[pl.BlockSpec((tm, d), lambda i: (i, 0)),
                  pl.BlockSpec((1, d), lambda i: (0, 0)),
                  pl.BlockSpec((1, 1, d), lambda i: (i // per_b, 0, 0)),
                  pl.BlockSpec((1, 1, d), lambda i: (i // per_b, 0, 0)),
                  pl.BlockSpec(wq_t.shape, lambda i: (0, 0)),
                  pl.BlockSpec(keys.shape, lambda i: (0, 0, 0, 0))],
        out_specs=[pl.BlockSpec((d, tm), lambda i: (0, i)), big, big, big, big],
        out_shape=[jax.ShapeDtypeStruct((d, t), BF16), big_f32, big_f32, big_bf16, big_bf16],
        compiler_params=_cparams(("parallel",)),
        name="peer_route",
    )(x2d, gain.reshape(1, d), shift, scale, wq_t, keys)


def _rows_bf16(row, n):
    tile = jnp.broadcast_to(row, (16, row.shape[1])).astype(BF16)
    return jnp.concatenate([tile] * (n // 16), axis=0)


def _peer_kernel(ht_ref, u_ref, vt_ref, cnt_ref, e1_ref, rank_ref, e2_ref, x_ref, g2_ref,
                 o_ref, acc_ref, *, te, tm):
    e = pl.program_id(1)
    nk = PEER_NKEYS
    blk = PEER_BLOCK
    rows = blk // nk
    dot = functools.partial(jnp.dot, preferred_element_type=F32)

    @pl.when(e == 0)
    def _():
        acc_ref[...] = jnp.zeros_like(acc_ref)

    nblk = te // blk

    def first(c):
        return dot(u_ref[c * blk:(c + 1) * blk, :], ht_ref[...])

    def gated(c, act):
        parts = []
        for ii in range(rows):
            i = (e * nblk + c) * rows + ii
            gate = jnp.zeros((nk, tm), BF16)
            for hd in range(PEER_HEADS):
                count = _rows_bf16(cnt_ref[hd, pl.ds(i, 1), :], nk)
                e1 = _rows_bf16(e1_ref[hd, pl.ds(i, 1), :], nk)
                gate = gate + jnp.where(rank_ref[hd] < count, e1 * e2_ref[hd], jnp.zeros((), BF16))
            parts.append(gate * _gelu(act[ii * nk:(ii + 1) * nk, :]).astype(BF16))
        return jnp.concatenate(parts, axis=0)

    acts = {0: first(0), 1: first(1)}
    for c in range(nblk):
        p = gated(c, acts.pop(c))
        if c + 2 < nblk:
            acts[c + 2] = first(c + 2)
        acc_ref[...] += dot(vt_ref[:, c * blk:(c + 1) * blk], p)

    @pl.when(e == pl.num_programs(1) - 1)
    def _():
        o_ref[...] = x_ref[...] + g2_ref[0] * acc_ref[...].T


def _peer(ht, u_bf, v_bf, cnt, e1, rank2, e2, x2d, g2, seq):
    t, d = x2d.shape
    ne = u_bf.shape[0]
    tm = min(512, seq)
    te = 1024
    per_b = seq // tm
    nh, nk = PEER_HEADS, PEER_NKEYS
    once = pl.Buffered(1)
    big = pl.BlockSpec((nh, nk, tm), lambda i, g: (0, 0, i), pipeline_mode=once)
    return pl.pallas_call(
        functools.partial(_peer_kernel, te=te, tm=tm),
        grid=(t // tm, ne // te),
        in_specs=[pl.BlockSpec((d, tm), lambda i, g: (0, i), pipeline_mode=once),
                  pl.BlockSpec((te, d), lambda i, g: (g, 0)),
                  pl.BlockSpec((d, te), lambda i, g: (0, g)),
                  big, big, big, big,
                  pl.BlockSpec((tm, d), lambda i, g: (i, 0), pipeline_mode=once),
                  pl.BlockSpec((1, 1, d), lambda i, g: (i // per_b, 0, 0))],
        out_specs=pl.BlockSpec((tm, d), lambda i, g: (i, 0)),
        out_shape=jax.ShapeDtypeStruct((t, d), F32),
        scratch_shapes=[pltpu.VMEM((d, tm), F32)],
        compiler_params=_cparams(("parallel", "arbitrary")),
        name="peer_experts",
    )(ht, u_bf, v_bf.T, cnt, e1, rank2, e2, x2d, g2)


def _final_kernel(x_ref, gain_ref, o_ref):
    x = x_ref[...]
    o_ref[...] = x * lax.rsqrt(jnp.mean(x * x, axis=-1, keepdims=True) + EPS) * gain_ref[...]


def _final_norm(x2d, gain):
    t, d = x2d.shape
    tm = 512
    return pl.pallas_call(
        _final_kernel,
        grid=(t // tm,),
        in_specs=[pl.BlockSpec((tm, d), lambda i: (i, 0)), pl.BlockSpec((1, d), lambda i: (0, 0))],
        out_specs=pl.BlockSpec((tm, d), lambda i: (i, 0)),
        out_shape=jax.ShapeDtypeStruct((t, d), F32),
        compiler_params=_cparams(("parallel",)),
        name="final_norm",
    )(x2d, gain.astype(F32).reshape(1, d))


def _layer(x2d, c, bsz, seq, ada_w, ada_b, norm_mix_gain, norm_ffn_gain, w_in, s5_params, s5_d,
           s5_w_glu, s5_b_glu, lower_bound, hg_norm_gain, pool_w, pool_scale, w_gate, w_branch, w_out,
           peer_w_query, peer_sub_keys, peer_u, peer_v):
    d = D_MODEL
    mod = _ada(c.astype(F32), ada_w.astype(F32), ada_b.astype(F32))
    sh1, sc1, g1, sh2, sc2, g2 = [m.reshape(bsz, 1, d) for m in jnp.split(mod, 6, axis=-1)]
    wcat = jnp.concatenate([w_gate[n] for n in range(N_BRANCH)] + [w_in], axis=1).astype(BF16)
    proj = _inproj(x2d, norm_mix_gain.astype(F32), sh1, sc1, wcat, seq)
    proj3 = proj.reshape(bsz, seq, CAT_COLS)
    bbd, cbd, tabs = _s5_tables(*s5_params)
    s5o = _s5(proj3, bbd, cbd, tabs, s5_d, s5_w_glu, s5_b_glu)
    hg = _hgrn(proj3, lower_bound, hg_norm_gain.astype(F32))
    sb = _stick_breaking(proj3)
    po = _pool(proj3, pool_w, pool_scale)
    t = bsz * seq
    flat = lambda a: a.reshape(t, BRANCH_WIDTH)
    x1 = _merge(x2d, g1, flat(s5o), flat(hg), flat(sb), flat(po), proj, w_branch, w_out, seq)
    ht, cnt, e1, rank2, e2 = _route(x1, norm_ffn_gain.astype(F32), sh2, sc2,
                                    peer_w_query.T.astype(BF16), peer_sub_keys.astype(BF16), seq)
    return _peer(ht, peer_u.astype(BF16), peer_v.astype(BF16), cnt, e1, rank2, e2, x1, g2, seq)


def kernel(x, c, ada_w, ada_b, norm_mix_gain, norm_ffn_gain, w_in, s5_lambda_re, s5_lambda_im, s5_log_step, s5_b_re, s5_b_im, s5_c_re, s5_c_im, s5_d, s5_w_glu, s5_b_glu, hg_lb_logits, hg_norm_gain, pool_w, pool_scale, w_gate, w_branch, w_out, peer_w_query, peer_sub_keys, peer_u, peer_v, final_gain):
    bsz, seq, d = x.shape
    depth = ada_w.shape[0]
    lb_soft = jax.nn.softmax(hg_lb_logits.astype(F32), axis=0)
    lower_bounds = jnp.cumsum(lb_soft, axis=0) - lb_soft[0:1]
    x2d = x.astype(F32).reshape(bsz * seq, d)
    for l in range(depth):
        s5_params = (s5_lambda_re[l], s5_lambda_im[l], s5_log_step[l], s5_b_re[l], s5_b_im[l],
                     s5_c_re[l], s5_c_im[l])
        x2d = _layer(x2d, c, bsz, seq, ada_w[l], ada_b[l], norm_mix_gain[l], norm_ffn_gain[l], w_in[l],
                     s5_params, s5_d[l], s5_w_glu[l], s5_b_glu[l], lower_bounds[l], hg_norm_gain[l],
                     pool_w[l], pool_scale[l], w_gate[l], w_branch[l], w_out[l], peer_w_query[l],
                     peer_sub_keys[l], peer_u[l], peer_v[l])
    return _final_norm(x2d, final_gain).reshape(bsz, seq, d).astype(x.dtype)
```

```python
import functools
import math

import jax
import jax.numpy as jnp
from jax import lax
from jax.experimental import pallas as pl
from jax.experimental.pallas import tpu as pltpu

F32 = jnp.float32
BF16 = jnp.bfloat16

D_MODEL = 2048
BRANCH_WIDTH = 512
N_BRANCH = 4
LANE = 128
S5_GROUP = 16
S5_GROUPS = 32
S5_STATE = 64
S5_MIN_NEG = 1e-4
S5_ROWS = 8
HG_HEADS = 4
HG_F_MIN = 1e-6
HG_CHUNK = 128
HG_SUB = 16
SB_HEADS = 4
SB_DIM = 128
SB_TILE = 256
POOL_WINDOWS = (2, 4, 8, 16)
PEER_HEADS = 8
PEER_NKEYS = 128
PEER_TOPK = 16
PEER_BLOCK = 256
EPS = 1e-6
NEG_BIG = -3.0e38
NO_RANK = 255.0

GATE_COLS = N_BRANCH * D_MODEL
IN_COLS = 9 * BRANCH_WIDTH
CAT_COLS = GATE_COLS + IN_COLS
PROJ_BLK0 = GATE_COLS // LANE
INPROJ_CHUNK = 256
VMEM_LIMIT = 48 * 1024 * 1024
PEER_VMEM_LIMIT = 56 * 1024 * 1024


def _cparams(semantics, vmem=VMEM_LIMIT):
    return pltpu.CompilerParams(dimension_semantics=semantics, vmem_limit_bytes=vmem)


def _split3(x):
    hi = x.astype(BF16)
    r1 = x - hi.astype(F32)
    mid = r1.astype(BF16)
    lo = (r1 - mid.astype(F32)).astype(BF16)
    return hi, mid, lo


def _dot01_left(m01, x):
    hi, mid, lo = _split3(x)
    d = functools.partial(jnp.dot, preferred_element_type=F32)
    return d(m01, hi) + d(m01, mid) + d(m01, lo)


def _dot01_right2(x, m01):
    hi = x.astype(BF16)
    lo = (x - hi.astype(F32)).astype(BF16)
    d = functools.partial(jnp.dot, preferred_element_type=F32)
    return d(hi, m01) + d(lo, m01)


def _dot_f32(a, b):
    ah, am, al = _split3(a)
    bh, bm, bl = _split3(b)
    d = functools.partial(jnp.dot, preferred_element_type=F32)
    return (d(ah, bh) + (d(ah, bm) + d(am, bh))) + ((d(ah, bl) + d(al, bh)) + d(am, bm))


def _gelu(x):
    return 0.5 * x * (1.0 + lax.erf(x * (1.0 / math.sqrt(2.0))))


def _sigmoid(x):
    return 0.5 * jnp.tanh(0.5 * x) + 0.5


def _silu(x):
    return x * _sigmoid(x)


def _rms_modulate(x, gain, shift, scale):
    y = x * lax.rsqrt(jnp.mean(x * x, axis=-1, keepdims=True) + EPS) * gain
    return y * (1.0 + scale) + shift


def _ada_kernel(c_ref, w_ref, b_ref, o_ref):
    o_ref[...] = _dot_f32(_silu(c_ref[...]), w_ref[...]) + b_ref[...]


def _ada(c, w, b):
    bsz, d = c.shape
    n = w.shape[1]
    tn = 1024
    return pl.pallas_call(
        _ada_kernel,
        grid=(n // tn,),
        in_specs=[pl.BlockSpec((bsz, d), lambda j: (0, 0)),
                  pl.BlockSpec((d, tn), lambda j: (0, j)),
                  pl.BlockSpec((1, tn), lambda j: (0, j))],
        out_specs=pl.BlockSpec((bsz, tn), lambda j: (0, j)),
        out_shape=jax.ShapeDtypeStruct((bsz, n), F32),
        compiler_params=_cparams(("arbitrary",)),
        name="ada_mod",
    )(c, w, b.reshape(1, n))


def _inproj_kernel(x_ref, gain_ref, sh_ref, sc_ref, w_ref, o_ref, h_ref, *, gate_chunks):
    j = pl.program_id(1)

    @pl.when(j == 0)
    def _():
        def rows(r, c):
            sl = pl.ds(pl.multiple_of(r * LANE, LANE), LANE)
            h_ref[sl, :] = _rms_modulate(x_ref[sl, :], gain_ref[...], sh_ref[0], sc_ref[0]).astype(BF16)
            return c

        lax.fori_loop(0, x_ref.shape[0] // LANE, rows, 0)

    cw = INPROJ_CHUNK
    nch = w_ref.shape[1] // cw
    mm = lambda k: jnp.dot(h_ref[...], w_ref[:, k * cw:(k + 1) * cw], preferred_element_type=F32)
    parts = {0: mm(0), 1: mm(1)}
    for k in range(nch):
        r = parts.pop(k)
        if k + 2 < nch:
            parts[k + 2] = mm(k + 2)
        is_gate = j * nch + k < gate_chunks
        o_ref[:, k * cw:(k + 1) * cw] = jnp.where(is_gate, _sigmoid(r), r).astype(o_ref.dtype)


def _inproj(x2d, gain, shift, scale, wcat, seq):
    t, d = x2d.shape
    n = wcat.shape[1]
    tm = min(1024, seq)
    tn = 5 * INPROJ_CHUNK
    per_b = seq // tm
    return pl.pallas_call(
        functools.partial(_inproj_kernel, gate_chunks=GATE_COLS // INPROJ_CHUNK),
        grid=(t // tm, n // tn),
        in_specs=[pl.BlockSpec((tm, d), lambda i, j: (i, 0)),
                  pl.BlockSpec((1, d), lambda i, j: (0, 0)),
                  pl.BlockSpec((1, 1, d), lambda i, j: (i // per_b, 0, 0)),
                  pl.BlockSpec((1, 1, d), lambda i, j: (i // per_b, 0, 0)),
                  pl.BlockSpec((d, tn), lambda i, j: (0, j))],
        out_specs=pl.BlockSpec((tm, tn), lambda i, j: (i, j)),
        out_shape=jax.ShapeDtypeStruct((t, n), BF16),
        scratch_shapes=[pltpu.VMEM((tm, d), BF16)],
        compiler_params=_cparams(("parallel", "arbitrary")),
        name="inproj",
    )(x2d, gain.reshape(1, d), shift, scale, wcat)


def _s5_tables(lam_re, lam_im, log_step, b_re, b_im, c_re, c_im):
    g, p, h = S5_GROUPS, S5_STATE, S5_GROUP
    lr = jnp.minimum(lam_re.astype(F32), -S5_MIN_NEG)
    li = lam_im.astype(F32)
    step = jnp.exp(log_step.astype(F32))[:, None]
    mag = jnp.exp(lr * step)
    ab_re = mag * jnp.cos(li * step)
    ab_im = mag * jnp.sin(li * step)
    den = lr * lr + li * li
    nr = ab_re - 1.0
    fr = (nr * lr + ab_im * li) / den
    fi = (ab_im * lr - nr * li) / den
    br, bi = b_re.astype(F32), b_im.astype(F32)
    bb_re = fr[..., None] * br - fi[..., None] * bi
    bb_im = fr[..., None] * bi + fi[..., None] * br
    eye_g = jnp.eye(g, dtype=F32)

    def in_map(m):
        return jnp.einsum('gph,gk->ghkp', m, eye_g).reshape(g * h, g * p)

    def out_map(m):
        return jnp.einsum('ghp,gk->gpkh', m, eye_g).reshape(g * p, g * h)

    bbd = jnp.concatenate([in_map(bb_re), in_map(bb_im)], axis=1).astype(BF16)
    cbd = jnp.concatenate([out_map(c_re.astype(F32)), out_map(-c_im.astype(F32))], axis=0).astype(BF16)
    prs, pis = [ab_re], [ab_im]
    for _ in range(S5_ROWS - 1):
        pr_, pi_ = prs[-1], pis[-1]
        prs.append(pr_ * ab_re - pi_ * ab_im)
        pis.append(pr_ * ab_im + pi_ * ab_re)
    rows = jnp.arange(S5_ROWS)[:, None]
    tabs = []
    for d in (1, 2, 4):
        tabs.append(jnp.where(rows >= d, prs[d - 1].reshape(1, g * p), 0.0))
        tabs.append(jnp.where(rows >= d, pis[d - 1].reshape(1, g * p), 0.0))
    tabs.append(jnp.stack(prs).reshape(S5_ROWS, g * p))
    tabs.append(jnp.stack(pis).reshape(S5_ROWS, g * p))
    return bbd, cbd, jnp.stack(tabs).astype(F32)


def _s5_kernel(u_ref, bbd_ref, tab_ref, cbd_ref, d_ref, wglu_ref, bglu_ref, o_ref,
               bu_scr, x_scr, carry_scr, *, tile):
    ns = S5_GROUPS * S5_STATE
    lc = 512

    @pl.when(pl.program_id(1) == 0)
    def _():
        carry_scr[...] = jnp.zeros_like(carry_scr)

    u = u_ref[0]
    bu_scr[...] = jnp.dot(u, bbd_ref[...], preferred_element_type=F32)

    def block(r, c):
        off = pl.multiple_of(r * S5_ROWS, S5_ROWS)
        for ch in range(ns // lc):
            re_l = slice(ch * lc, (ch + 1) * lc)
            im_l = slice(ns + ch * lc, ns + (ch + 1) * lc)
            xr = bu_scr[pl.ds(off, S5_ROWS), re_l]
            xi = bu_scr[pl.ds(off, S5_ROWS), im_l]
            for k, d in enumerate((1, 2, 4)):
                sr = pltpu.roll(xr, d, 0)
                si = pltpu.roll(xi, d, 0)
                mr = tab_ref[2 * k, :, re_l]
                mi = tab_ref[2 * k + 1, :, re_l]
                xr, xi = xr + (mr * sr - mi * si), xi + (mr * si + mi * sr)
            ar = tab_ref[6, :, re_l]
            ai = tab_ref[7, :, re_l]
            cr = carry_scr[:, re_l]
            ci = carry_scr[:, im_l]
            xr, xi = xr + (ar * cr - ai * ci), xi + (ar * ci + ai * cr)
            x_scr[pl.ds(off, S5_ROWS), re_l] = xr
            x_scr[pl.ds(off, S5_ROWS), im_l] = xi
            carry_scr[:, re_l] = jnp.broadcast_to(xr[S5_ROWS - 1:S5_ROWS], (S5_ROWS, lc))
            carry_scr[:, im_l] = jnp.broadcast_to(xi[S5_ROWS - 1:S5_ROWS], (S5_ROWS, lc))
        return c

    lax.fori_loop(0, tile // S5_ROWS, block, 0)
    y = jnp.dot(x_scr[...].astype(BF16), cbd_ref[...], preferred_element_type=F32)
    y = _gelu(y + d_ref[...] * u.astype(F32))
    gate = _sigmoid(jnp.dot(y.astype(BF16), wglu_ref[...], preferred_element_type=F32) + bglu_ref[...])
    o_ref[0] = (y * gate).astype(o_ref.dtype)


def _s5(proj3, bbd, cbd, tabs, d_skip, w_glu, b_glu):
    bsz, seq, _ = proj3.shape
    bw = BRANCH_WIDTH
    ns = S5_GROUPS * S5_STATE
    tile = min(256, seq)
    const = lambda shape: pl.BlockSpec(shape, lambda b, i: (0,) * len(shape))
    return pl.pallas_call(
        functools.partial(_s5_kernel, tile=tile),
        grid=(bsz, seq // tile),
        in_specs=[pl.BlockSpec((1, tile, bw), lambda b, i: (b, i, GATE_COLS // bw)),
                  const((bw, 2 * ns)), const((8, S5_ROWS, ns)), const((2 * ns, bw)),
                  const((1, bw)), const((bw, bw)), const((1, bw))],
        out_specs=pl.BlockSpec((1, tile, bw), lambda b, i: (b, i, 0)),
        out_shape=jax.ShapeDtypeStruct((bsz, seq, bw), BF16),
        scratch_shapes=[pltpu.VMEM((tile, 2 * ns), F32), pltpu.VMEM((tile, 2 * ns), F32),
                        pltpu.VMEM((S5_ROWS, 2 * ns), F32)],
        compiler_params=_cparams(("parallel", "arbitrary")),
        name="s5",
    )(proj3, bbd, tabs, cbd, d_skip.astype(F32).reshape(1, bw), w_glu.astype(BF16),
      b_glu.astype(F32).reshape(1, bw))


def _hgrn_kernel(q_ref, f_ref, i_ref, g_ref, lb_ref, gain_ref, tril_ref, o_ref, st_scr, *, seq):
    c, sub = HG_CHUNK, HG_SUB
    lb = lb_ref[0]
    gain = gain_ref[0]
    tril = tril_ref[...]
    row = lax.broadcasted_iota(jnp.int32, (c, LANE), 0)
    rsub = jnp.bitwise_and(row, sub - 1)
    tcol = lax.broadcasted_iota(jnp.int32, (sub, c), 1)
    st_scr[...] = jnp.zeros_like(st_scr)
    nt = (((1,), (1,)), ((), ()))

    def chunk(ci, carry):
        off = pl.multiple_of(ci * c, c)
        q = _silu(q_ref[0, pl.ds(off, c), :].astype(F32))
        fg = lb + (1.0 - lb) * _sigmoid(f_ref[0, pl.ds(off, c), :].astype(F32))
        logf = jnp.log(jnp.maximum(fg, HG_F_MIN))
        k = 1.0 - fg
        v = i_ref[0, pl.ds(off, c), :]
        vf = v.astype(F32)
        bc = _dot01_left(tril, logf)
        o = jnp.zeros((c, LANE), F32)
        for lag in range(sub):
            if lag == 0:
                kd, bd, vd = k, bc, vf
            else:
                kd = pltpu.roll(k, lag, 0)
                bd = pltpu.roll(bc, lag, 0)
                vd = pltpu.roll(vf, lag, 0)
            valid = rsub >= lag
            e = jnp.exp(jnp.where(valid, bc - bd, 0.0))
            w = jnp.sum(q * kd * e, axis=-1, keepdims=True)
            o = o + jnp.where(valid, w * vd, 0.0)
        parts = []
        for j in range(c // sub - 1):
            lo, hi = sub * j, sub * (j + 1)
            bend = bc[hi - 1:hi, :]
            qj = (q * jnp.exp(jnp.minimum(bc - bend, 0.0))).astype(BF16)
            kj = (k[lo:hi] * jnp.exp(bend - bc[lo:hi])).astype(BF16)
            st = lax.dot_general(kj, qj, nt, preferred_element_type=F32)
            parts.append(jnp.where(tcol >= hi, st, 0.0))
        parts.append(jnp.zeros((sub, c), F32))
        scores = jnp.concatenate(parts, axis=0).T
        o = o + jnp.dot(scores.astype(BF16), v, preferred_element_type=F32)
        st_prev = st_scr[...]
        qe = (q * jnp.exp(bc)).astype(BF16)
        o = o + lax.dot_general(qe, st_prev.astype(BF16), nt, preferred_element_type=F32)
        blast = bc[c - 1:c, :]
        kdec = (k * jnp.exp(blast - bc)).astype(BF16)
        st_scr[...] = st_prev * jnp.exp(blast) + jnp.dot(vf.T.astype(BF16), kdec,
                                                         preferred_element_type=F32)
        o = o * lax.rsqrt(jnp.mean(o * o, axis=-1, keepdims=True) + EPS) * gain
        o = o * _silu(g_ref[0, pl.ds(off, c), :].astype(F32))
        o_ref[0, pl.ds(off, c), :] = o.astype(o_ref.dtype)
        return carry

    lax.fori_loop(0, seq // c, chunk, 0)


def _hgrn(proj3, lower_bound, norm_gain):
    bsz, seq, _ = proj3.shape
    c = HG_CHUNK
    tril = jnp.tril(jnp.ones((c, c), F32)).astype(BF16)
    base = PROJ_BLK0 + BRANCH_WIDTH // LANE

    def col(k):
        return pl.BlockSpec((1, seq, LANE), lambda b, h: (b, 0, base + k * HG_HEADS + h))

    vec = pl.BlockSpec((1, 1, LANE), lambda b, h: (h, 0, 0))
    return pl.pallas_call(
        functools.partial(_hgrn_kernel, seq=seq),
        grid=(bsz, HG_HEADS),
        in_specs=[col(0), col(1), col(2), col(3), vec, vec,
                  pl.BlockSpec((c, c), lambda b, h: (0, 0))],
        out_specs=pl.BlockSpec((1, seq, LANE), lambda b, h: (b, 0, h)),
        out_shape=jax.ShapeDtypeStruct((bsz, seq, BRANCH_WIDTH), BF16),
        scratch_shapes=[pltpu.VMEM((LANE, LANE), F32)],
        compiler_params=_cparams(("parallel", "parallel")),
        name="hgrn2",
    )(proj3, proj3, proj3, proj3, lower_bound.reshape(HG_HEADS, 1, LANE),
      norm_gain.reshape(HG_HEADS, 1, LANE), tril)


def _sb_kernel(q_ref, k_ref, v_ref, up_ref, o_ref, acc_scr, later_scr):
    qi = pl.program_id(1)
    t = SB_TILE
    up = up_ref[...]
    scale = 1.0 / math.sqrt(SB_DIM)
    row = lax.broadcasted_iota(jnp.int32, (t, t), 0)
    colv = lax.broadcasted_iota(jnp.int32, (t, t), 1)
    below = colv < row
    nt = (((1,), (1,)), ((), ()))
    acc_scr[...] = jnp.zeros_like(acc_scr)
    later_scr[...] = jnp.zeros_like(later_scr)

    def body(jj, carry):
        off = pl.multiple_of((qi - jj) * t, t)
        mask = jnp.logical_or(jj > 0, below)
        heads = [slice(h * SB_DIM, (h + 1) * SB_DIM) for h in range(SB_HEADS)]
        zs = [lax.dot_general(q_ref[0, :, hl], k_ref[0, pl.ds(off, t), hl], nt,
                              preferred_element_type=F32) * scale for hl in heads]
        logits, suffixes, totals = [], [], []
        for h in range(SB_HEADS):
            z = zs[h]
            sp = jnp.maximum(z, 0.0) + jnp.log(1.0 + jnp.exp(-jnp.abs(z)))
            log_not = jnp.where(mask, -sp, 0.0)
            suffixes.append(_dot01_right2(log_not, up))
            logits.append(z - sp)
            totals.append(jnp.sum(log_not, axis=-1, keepdims=True))
        for h in range(SB_HEADS):
            later = later_scr[h]
            w = jnp.where(mask, jnp.exp(logits[h] + suffixes[h] + later), 0.0)
            acc_scr[h] += jnp.dot(w.astype(BF16), v_ref[0, pl.ds(off, t), heads[h]],
                                  preferred_element_type=F32)
            later_scr[h] = later + totals[h]
        return carry

    lax.fori_loop(0, qi + 1, body, 0)
    for h in range(SB_HEADS):
        o_ref[0, :, h * SB_DIM:(h + 1) * SB_DIM] = acc_scr[h].astype(o_ref.dtype)


def _stick_breaking(proj3):
    bsz, seq, _ = proj3.shape
    t = SB_TILE
    bw = BRANCH_WIDTH
    base = (GATE_COLS + 5 * bw) // bw
    up = (jnp.arange(t)[:, None] > jnp.arange(t)[None, :]).astype(BF16)
    return pl.pallas_call(
        _sb_kernel,
        grid=(bsz, seq // t),
        in_specs=[pl.BlockSpec((1, t, bw), lambda b, i: (b, i, base)),
                  pl.BlockSpec((1, seq, bw), lambda b, i: (b, 0, base + 1)),
                  pl.BlockSpec((1, seq, bw), lambda b, i: (b, 0, base + 2)),
                  pl.BlockSpec((t, t), lambda b, i: (0, 0))],
        out_specs=pl.BlockSpec((1, t, bw), lambda b, i: (b, i, 0)),
        out_shape=jax.ShapeDtypeStruct((bsz, seq, bw), BF16),
        scratch_shapes=[pltpu.VMEM((SB_HEADS, t, SB_DIM), F32), pltpu.VMEM((SB_HEADS, t, 1), F32)],
        compiler_params=_cparams(("parallel", "arbitrary")),
        name="stick_breaking",
    )(proj3, proj3, proj3, up)


def _pool_kernel(p_ref, bc_ref, bp_ref, w_ref, sc_ref, o_ref, *, seq):
    g = pl.program_id(1)
    tile = LANE
    window = lax.shift_left(jnp.int32(2), g)
    row = lax.broadcasted_iota(jnp.int32, (tile, LANE), 0)
    band_cur = bc_ref[0]
    band_prev = bp_ref[0]
    wmix = w_ref[0]
    scale = sc_ref[0]

    def body(i, carry):
        off = pl.multiple_of(i * tile, tile)
        poff = pl.multiple_of(jnp.maximum(i - 1, 0) * tile, tile)
        cur = p_ref[0, pl.ds(off, tile), :]
        prev = p_ref[0, pl.ds(poff, tile), :]
        win = jnp.dot(band_cur, cur, preferred_element_type=F32)
        win = win + jnp.where(i > 0, jnp.dot(band_prev, prev, preferred_element_type=F32), 0.0)
        count = jnp.minimum(row + (off + 1), window).astype(F32)
        pooled = win / count - cur.astype(F32)
        mixed = jnp.dot(pooled.astype(BF16), wmix, preferred_element_type=F32) * scale
        o_ref[0, pl.ds(off, tile), :] = mixed.astype(o_ref.dtype)
        return carry

    lax.fori_loop(0, seq // tile, body, 0)


def _pool(proj3, pool_w, pool_scale):
    bsz, seq, _ = proj3.shape
    ng = len(POOL_WINDOWS)
    base = PROJ_BLK0 + 8 * BRANCH_WIDTH // LANE
    t = jnp.arange(LANE)
    lag = t[:, None] - t[None, :]
    wins = jnp.asarray(POOL_WINDOWS)[:, None, None]
    band_cur = ((lag[None] >= 0) & (lag[None] < wins)).astype(BF16)
    band_prev = (((lag[None] + LANE) >= 0) & ((lag[None] + LANE) < wins)).astype(BF16)
    mat = pl.BlockSpec((1, LANE, LANE), lambda b, g: (g, 0, 0))
    return pl.pallas_call(
        functools.partial(_pool_kernel, seq=seq),
        grid=(bsz, ng),
        in_specs=[pl.BlockSpec((1, seq, LANE), lambda b, g: (b, 0, base + g)),
                  mat, mat, mat,
                  pl.BlockSpec((1, 1, LANE), lambda b, g: (g, 0, 0))],
        out_specs=pl.BlockSpec((1, seq, LANE), lambda b, g: (b, 0, g)),
        out_shape=jax.ShapeDtypeStruct((bsz, seq, BRANCH_WIDTH), BF16),
        compiler_params=_cparams(("parallel", "parallel")),
        name="pool",
    )(proj3, band_cur, band_prev, pool_w.astype(BF16), pool_scale.astype(F32).reshape(ng, 1, LANE))


def _merge_kernel(x_ref, g1_ref, s5_ref, hg_ref, sb_ref, po_ref, gt0_ref, gt1_ref, gt2_ref, gt3_ref,
                  wbr_ref, wout_ref, o_ref):
    dot = functools.partial(jnp.dot, preferred_element_type=F32)
    branches = (s5_ref, hg_ref, sb_ref, po_ref)
    gates = (gt0_ref, gt1_ref, gt2_ref, gt3_ref)
    merged = None
    for n in range(N_BRANCH):
        term = gates[n][...].astype(F32) * dot(branches[n][...], wbr_ref[n])
        merged = term if merged is None else merged + term
    mix = dot(merged.astype(BF16), wout_ref[...])
    o_ref[...] = x_ref[...] + g1_ref[0] * mix


def _merge(x2d, g1, s5o, hg, sb, po, proj, w_branch, w_out, seq):
    t, d = x2d.shape
    bw = BRANCH_WIDTH
    tm = min(256, seq)
    per_b = seq // tm
    tok = lambda w: pl.BlockSpec((tm, w), lambda i: (i, 0))
    gate = lambda n: pl.BlockSpec((tm, d), lambda i, n=n: (i, n))
    const = lambda shape: pl.BlockSpec(shape, lambda i: (0,) * len(shape))
    return pl.pallas_call(
        _merge_kernel,
        grid=(t // tm,),
        in_specs=[tok(d), pl.BlockSpec((1, 1, d), lambda i: (i // per_b, 0, 0)),
                  tok(bw), tok(bw), tok(bw), tok(bw),
                  gate(0), gate(1), gate(2), gate(3),
                  const((N_BRANCH, bw, d)), const((d, d))],
        out_specs=tok(d),
        out_shape=jax.ShapeDtypeStruct((t, d), F32),
        compiler_params=_cparams(("parallel",)),
        name="merge",
    )(x2d, g1, s5o, hg, sb, po, proj, proj, proj, proj, w_branch.astype(BF16), w_out.astype(BF16))


def _top_values(s, n, with_rank=False):
    vals = []
    work = s
    rank = jnp.full(s.shape, NO_RANK, F32) if with_rank else None
    for r in range(n):
        m = jnp.max(work, axis=0, keepdims=True)
        vals.append(m)
        hit = work >= m
        if with_rank:
            rank = jnp.where(hit, float(r), rank)
        if with_rank or r + 1 < n:
            work = jnp.where(hit, NEG_BIG, work)
    return vals, rank


def _route_kernel(x_ref, gain_ref, sh_ref, sc_ref, wq_ref, keys_ref,
                  ht_ref, cnt_ref, e1_ref, rank_ref, e2_ref):
    k = PEER_TOPK
    h = _rms_modulate(x_ref[...], gain_ref[...], sh_ref[0], sc_ref[0])
    ht = h.T.astype(BF16)
    ht_ref[...] = ht
    qt = jnp.dot(wq_ref[...], ht, preferred_element_type=F32)
    for hd in range(PEER_HEADS):
        sc = []
        for half in range(2):
            lo = (2 * hd + half) * PEER_NKEYS
            qs = qt[lo:lo + PEER_NKEYS, :].astype(BF16)
            sc.append(jnp.dot(keys_ref[hd, half], qs, preferred_element_type=F32))
        top1, _ = _top_values(sc[0], k)
        top2, rank2 = _top_values(sc[1], k, with_rank=True)
        v1 = jnp.concatenate(top1, axis=0)
        v2 = jnp.concatenate(top2, axis=0)
        cand = [top1[0] + v2] + [top1[a] + v2[:8] for a in range(1, 8)] + [v1[8:] + top2[0]]
        best, _ = _top_values(jnp.concatenate(cand, axis=0), k)
        tau = best[k - 1]
        z = None
        for r in range(k):
            e = jnp.exp(best[r] - best[0])
            z = e if z is None else z + e
        count = jnp.zeros(sc[0].shape, F32)
        for b in range(k):
            count = count + jnp.where(sc[0] + top2[b] >= tau, 1.0, 0.0)
        cnt_ref[hd] = count
        e1_ref[hd] = jnp.exp(sc[0] - top1[0]) / z
        rank_ref[hd] = rank2.astype(BF16)
        e2_ref[hd] = jnp.exp(sc[1] - top2[0]).astype(BF16)


def _route(x2d, gain, shift, scale, wq_t, keys, seq):
    t, d = x2d.shape
    tm = min(256, seq)
    per_b = seq // tm
    nh, nk = PEER_HEADS, PEER_NKEYS
    big = pl.BlockSpec((nh, nk, tm), lambda i: (0, 0, i))
    big_f32 = jax.ShapeDtypeStruct((nh, nk, t), F32)
    big_bf16 = jax.ShapeDtypeStruct((nh, nk, t), BF16)
    return pl.pallas_call(
        _route_kernel,
        grid=(t // tm,),
        in_specs=[pl.BlockSpec((tm, d), lambda i: (i, 0)),
                  pl.BlockSpec((1, d), lambda i: (0, 0)),
                  pl.BlockSpec((1, 1, d), lambda i: (i // per_b, 0, 0)),
                  pl.BlockSpec((1, 1, d), lambda i: (i // per_b, 0, 0)),
                  pl.BlockSpec(wq_t.shape, lambda i: (0, 0)),
                  pl.BlockSpec(keys.shape, lambda i: (0, 0, 0, 0))],
        out_specs=[pl.BlockSpec((d, tm), lambda i: (0, i)), big, big, big, big],
        out_shape=[jax.ShapeDtypeStruct((d, t), BF16), big_f32, big_f32, big_bf16, big_bf16],
        compiler_params=_cparams(("parallel",)),
        name="peer_route",
    )(x2d, gain.reshape(1, d), shift, scale, wq_t, keys)


def _rows_bf16(row, n):
    tile = jnp.broadcast_to(row, (16, row.shape[1])).astype(BF16)
    return jnp.concatenate([tile] * (n // 16), axis=0)


def _peer_kernel(ht_ref, u_ref, vt_ref, cnt_ref, e1_ref, rank_ref, e2_ref, x_ref, g2_ref, fg_ref,
                 o_ref, acc_ref, *, te, tm, final_norm):
    e = pl.program_id(1)
    nk = PEER_NKEYS
    blk = PEER_BLOCK
    rows = blk // nk
    dot = functools.partial(jnp.dot, preferred_element_type=F32)

    @pl.when(e == 0)
    def _():
        acc_ref[...] = jnp.zeros_like(acc_ref)

    nblk = te // blk

    def first(c):
        return dot(u_ref[c * blk:(c + 1) * blk, :], ht_ref[...])

    def gated(c, act):
        parts = []
        for ii in range(rows):
            i = (e * nblk + c) * rows + ii
            gate = jnp.zeros((nk, tm), BF16)
            for hd in range(PEER_HEADS):
                count = _rows_bf16(cnt_ref[hd, pl.ds(i, 1), :], nk)
                e1 = _rows_bf16(e1_ref[hd, pl.ds(i, 1), :], nk)
                gate = gate + jnp.where(rank_ref[hd] < count, e1 * e2_ref[hd], jnp.zeros((), BF16))
            parts.append(gate * _gelu(act[ii * nk:(ii + 1) * nk, :]).astype(BF16))
        return jnp.concatenate(parts, axis=0)

    acts = {0: first(0), 1: first(1)}
    for c in range(nblk):
        p = gated(c, acts.pop(c))
        if c + 2 < nblk:
            acts[c + 2] = first(c + 2)
        acc_ref[...] += dot(vt_ref[:, c * blk:(c + 1) * blk], p)

    @pl.when(e == pl.num_programs(1) - 1)
    def _():
        y = x_ref[...] + g2_ref[0] * acc_ref[...].T
        if final_norm:
            y = y * lax.rsqrt(jnp.mean(y * y, axis=-1, keepdims=True) + EPS) * fg_ref[...]
        o_ref[...] = y


def _peer(ht, u_bf, v_bf, cnt, e1, rank2, e2, x2d, g2, final_gain, final_norm, seq):
    t, d = x2d.shape
    ne = u_bf.shape[0]
    tm = min(512, seq)
    te = 1024
    per_b = seq // tm
    nh, nk = PEER_HEADS, PEER_NKEYS
    big = pl.BlockSpec((nh, nk, tm), lambda i, g: (0, 0, i))
    return pl.pallas_call(
        functools.partial(_peer_kernel, te=te, tm=tm, final_norm=final_norm),
        grid=(t // tm, ne // te),
        in_specs=[pl.BlockSpec((d, tm), lambda i, g: (0, i)),
                  pl.BlockSpec((te, d), lambda i, g: (g, 0)),
                  pl.BlockSpec((d, te), lambda i, g: (0, g)),
                  big, big, big, big,
                  pl.BlockSpec((tm, d), lambda i, g: (i, 0), pipeline_mode=pl.Buffered(1)),
                  pl.BlockSpec((1, 1, d), lambda i, g: (i // per_b, 0, 0)),
                  pl.BlockSpec((1, d), lambda i, g: (0, 0))],
        out_specs=pl.BlockSpec((tm, d), lambda i, g: (i, 0)),
        out_shape=jax.ShapeDtypeStruct((t, d), F32),
        scratch_shapes=[pltpu.VMEM((d, tm), F32)],
        compiler_params=_cparams(("parallel", "arbitrary"), vmem=PEER_VMEM_LIMIT),
        name="peer_experts",
    )(ht, u_bf, v_bf.T, cnt, e1, rank2, e2, x2d, g2, final_gain.astype(F32).reshape(1, d))


def _layer(x2d, c, bsz, seq, ada_w, ada_b, norm_mix_gain, norm_ffn_gain, w_in, s5_params, s5_d,
           s5_w_glu, s5_b_glu, lower_bound, hg_norm_gain, pool_w, pool_scale, w_gate, w_branch, w_out,
           peer_w_query, peer_sub_keys, peer_u, peer_v, final_gain, last):
    d = D_MODEL
    mod = _ada(c.astype(F32), ada_w.astype(F32), ada_b.astype(F32))
    sh1, sc1, g1, sh2, sc2, g2 = [m.reshape(bsz, 1, d) for m in jnp.split(mod, 6, axis=-1)]
    wcat = jnp.concatenate([w_gate[n] for n in range(N_BRANCH)] + [w_in], axis=1).astype(BF16)
    proj = _inproj(x2d, norm_mix_gain.astype(F32), sh1, sc1, wcat, seq)
    proj3 = proj.reshape(bsz, seq, CAT_COLS)
    bbd, cbd, tabs = _s5_tables(*s5_params)
    s5o = _s5(proj3, bbd, cbd, tabs, s5_d, s5_w_glu, s5_b_glu)
    hg = _hgrn(proj3, lower_bound, hg_norm_gain.astype(F32))
    sb = _stick_breaking(proj3)
    po = _pool(proj3, pool_w, pool_scale)
    t = bsz * seq
    flat = lambda a: a.reshape(t, BRANCH_WIDTH)
    x1 = _merge(x2d, g1, flat(s5o), flat(hg), flat(sb), flat(po), proj, w_branch, w_out, seq)
    ht, cnt, e1, rank2, e2 = _route(x1, norm_ffn_gain.astype(F32), sh2, sc2,
                                    peer_w_query.T.astype(BF16), peer_sub_keys.astype(BF16), seq)
    return _peer(ht, peer_u.astype(BF16), peer_v.astype(BF16), cnt, e1, rank2, e2, x1, g2, final_gain, last,
                 seq)


def kernel(x, c, ada_w, ada_b, norm_mix_gain, norm_ffn_gain, w_in, s5_lambda_re, s5_lambda_im, s5_log_step, s5_b_re, s5_b_im, s5_c_re, s5_c_im, s5_d, s5_w_glu, s5_b_glu, hg_lb_logits, hg_norm_gain, pool_w, pool_scale, w_gate, w_branch, w_out, peer_w_query, peer_sub_keys, peer_u, peer_v, final_gain):
    bsz, seq, d = x.shape
    depth = ada_w.shape[0]
    lb_soft = jax.nn.softmax(hg_lb_logits.astype(F32), axis=0)
    lower_bounds = jnp.cumsum(lb_soft, axis=0) - lb_soft[0:1]
    x2d = x.astype(F32).reshape(bsz * seq, d)
    for l in range(depth):
        s5_params = (s5_lambda_re[l], s5_lambda_im[l], s5_log_step[l], s5_b_re[l], s5_b_im[l],
                     s5_c_re[l], s5_c_im[l])
        x2d = _layer(x2d, c, bsz, seq, ada_w[l], ada_b[l], norm_mix_gain[l], norm_ffn_gain[l], w_in[l],
                     s5_params, s5_d[l], s5_w_glu[l], s5_b_glu[l], lower_bounds[l], hg_norm_gain[l],
                     pool_w[l], pool_scale[l], w_gate[l], w_branch[l], w_out[l], peer_w_query[l],
                     peer_sub_keys[l], peer_u[l], peer_v[l], final_gain, l == depth - 1)
    return x2d.reshape(bsz, seq, d).astype(x.dtype)
```

```python
import functools
import math

import jax
import jax.numpy as jnp
from jax import lax
from jax.experimental import pallas as pl
from jax.experimental.pallas import tpu as pltpu

F32 = jnp.float32
BF16 = jnp.bfloat16

D_MODEL = 2048
BRANCH_WIDTH = 512
N_BRANCH = 4
LANE = 128
S5_GROUP = 16
S5_GROUPS = 32
S5_STATE = 64
S5_MIN_NEG = 1e-4
S5_ROWS = 8
HG_HEADS = 4
HG_F_MIN = 1e-6
HG_CHUNK = 128
HG_SUB = 16
SB_HEADS = 4
SB_DIM = 128
SB_TILE = 256
POOL_WINDOWS = (2, 4, 8, 16)
PEER_HEADS = 8
PEER_NKEYS = 128
PEER_TOPK = 16
PEER_BLOCK = 256
EPS = 1e-6
NEG_BIG = -3.0e38

GATE_COLS = N_BRANCH * D_MODEL
IN_COLS = 9 * BRANCH_WIDTH
CAT_COLS = GATE_COLS + IN_COLS
PROJ_BLK0 = GATE_COLS // LANE
INPROJ_CHUNK = 256
VMEM_LIMIT = 48 * 1024 * 1024
PEER_VMEM_LIMIT = 56 * 1024 * 1024


def _cparams(semantics, vmem=VMEM_LIMIT):
    return pltpu.CompilerParams(dimension_semantics=semantics, vmem_limit_bytes=vmem)


def _split3(x):
    hi = x.astype(BF16)
    r1 = x - hi.astype(F32)
    mid = r1.astype(BF16)
    lo = (r1 - mid.astype(F32)).astype(BF16)
    return hi, mid, lo


def _dot01_left(m01, x):
    hi, mid, lo = _split3(x)
    d = functools.partial(jnp.dot, preferred_element_type=F32)
    return d(m01, hi) + d(m01, mid) + d(m01, lo)


def _dot01_right2(x, m01):
    hi = x.astype(BF16)
    lo = (x - hi.astype(F32)).astype(BF16)
    d = functools.partial(jnp.dot, preferred_element_type=F32)
    return d(hi, m01) + d(lo, m01)


def _dot_f32(a, b):
    ah, am, al = _split3(a)
    bh, bm, bl = _split3(b)
    d = functools.partial(jnp.dot, preferred_element_type=F32)
    return (d(ah, bh) + (d(ah, bm) + d(am, bh))) + ((d(ah, bl) + d(al, bh)) + d(am, bm))


def _gelu(x):
    return 0.5 * x * (1.0 + lax.erf(x * (1.0 / math.sqrt(2.0))))


def _sigmoid(x):
    return 0.5 * jnp.tanh(0.5 * x) + 0.5


def _silu(x):
    return x * _sigmoid(x)


def _rms_modulate(x, gain, shift, scale):
    y = x * lax.rsqrt(jnp.mean(x * x, axis=-1, keepdims=True) + EPS) * gain
    return y * (1.0 + scale) + shift


def _ada_kernel(c_ref, w_ref, b_ref, o_ref):
    o_ref[...] = _dot_f32(_silu(c_ref[...]), w_ref[...]) + b_ref[...]


def _ada(c, w, b):
    bsz, d = c.shape
    n = w.shape[1]
    tn = 1024
    return pl.pallas_call(
        _ada_kernel,
        grid=(n // tn,),
        in_specs=[pl.BlockSpec((bsz, d), lambda j: (0, 0)),
                  pl.BlockSpec((d, tn), lambda j: (0, j)),
                  pl.BlockSpec((1, tn), lambda j: (0, j))],
        out_specs=pl.BlockSpec((bsz, tn), lambda j: (0, j)),
        out_shape=jax.ShapeDtypeStruct((bsz, n), F32),
        compiler_params=_cparams(("arbitrary",)),
        name="ada_mod",
    )(c, w, b.reshape(1, n))


def _inproj_kernel(x_ref, gain_ref, sh_ref, sc_ref, w_ref, o_ref, h_ref, *, gate_chunks):
    j = pl.program_id(1)

    @pl.when(j == 0)
    def _():
        def rows(r, c):
            sl = pl.ds(pl.multiple_of(r * LANE, LANE), LANE)
            h_ref[sl, :] = _rms_modulate(x_ref[sl, :], gain_ref[...], sh_ref[0], sc_ref[0]).astype(BF16)
            return c

        lax.fori_loop(0, x_ref.shape[0] // LANE, rows, 0)

    cw = INPROJ_CHUNK
    nch = w_ref.shape[1] // cw
    mm = lambda k: jnp.dot(h_ref[...], w_ref[:, k * cw:(k + 1) * cw], preferred_element_type=F32)
    parts = {0: mm(0), 1: mm(1)}
    for k in range(nch):
        r = parts.pop(k)
        if k + 2 < nch:
            parts[k + 2] = mm(k + 2)
        is_gate = j * nch + k < gate_chunks
        o_ref[:, k * cw:(k + 1) * cw] = jnp.where(is_gate, _sigmoid(r), r).astype(o_ref.dtype)


def _inproj(x2d, gain, shift, scale, wcat, seq):
    t, d = x2d.shape
    n = wcat.shape[1]
    tm = min(1024, seq)
    tn = 5 * INPROJ_CHUNK
    per_b = seq // tm
    return pl.pallas_call(
        functools.partial(_inproj_kernel, gate_chunks=GATE_COLS // INPROJ_CHUNK),
        grid=(t // tm, n // tn),
        in_specs=[pl.BlockSpec((tm, d), lambda i, j: (i, 0)),
                  pl.BlockSpec((1, d), lambda i, j: (0, 0)),
                  pl.BlockSpec((1, 1, d), lambda i, j: (i // per_b, 0, 0)),
                  pl.BlockSpec((1, 1, d), lambda i, j: (i // per_b, 0, 0)),
                  pl.BlockSpec((d, tn), lambda i, j: (0, j))],
        out_specs=pl.BlockSpec((tm, tn), lambda i, j: (i, j)),
        out_shape=jax.ShapeDtypeStruct((t, n), BF16),
        scratch_shapes=[pltpu.VMEM((tm, d), BF16)],
        compiler_params=_cparams(("parallel", "arbitrary")),
        name="inproj",
    )(x2d, gain.reshape(1, d), shift, scale, wcat)


def _s5_tables(lam_re, lam_im, log_step, b_re, b_im, c_re, c_im):
    g, p, h = S5_GROUPS, S5_STATE, S5_GROUP
    lr = jnp.minimum(lam_re.astype(F32), -S5_MIN_NEG)
    li = lam_im.astype(F32)
    step = jnp.exp(log_step.astype(F32))[:, None]
    mag = jnp.exp(lr * step)
    ab_re = mag * jnp.cos(li * step)
    ab_im = mag * jnp.sin(li * step)
    den = lr * lr + li * li
    nr = ab_re - 1.0
    fr = (nr * lr + ab_im * li) / den
    fi = (ab_im * lr - nr * li) / den
    br, bi = b_re.astype(F32), b_im.astype(F32)
    bb_re = fr[..., None] * br - fi[..., None] * bi
    bb_im = fr[..., None] * bi + fi[..., None] * br
    eye_g = jnp.eye(g, dtype=F32)

    def in_map(m):
        return jnp.einsum('gph,gk->ghkp', m, eye_g).reshape(g * h, g * p)

    def out_map(m):
        return jnp.einsum('ghp,gk->gpkh', m, eye_g).reshape(g * p, g * h)

    bbd = jnp.concatenate([in_map(bb_re), in_map(bb_im)], axis=1).astype(BF16)
    cbd = jnp.concatenate([out_map(c_re.astype(F32)), out_map(-c_im.astype(F32))], axis=0).astype(BF16)
    prs, pis = [ab_re], [ab_im]
    for _ in range(S5_ROWS - 1):
        pr_, pi_ = prs[-1], pis[-1]
        prs.append(pr_ * ab_re - pi_ * ab_im)
        pis.append(pr_ * ab_im + pi_ * ab_re)
    rows = jnp.arange(S5_ROWS)[:, None]
    tabs = []
    for d in (1, 2, 4):
        tabs.append(jnp.where(rows >= d, prs[d - 1].reshape(1, g * p), 0.0))
        tabs.append(jnp.where(rows >= d, pis[d - 1].reshape(1, g * p), 0.0))
    tabs.append(jnp.stack(prs).reshape(S5_ROWS, g * p))
    tabs.append(jnp.stack(pis).reshape(S5_ROWS, g * p))
    return bbd, cbd, jnp.stack(tabs).astype(F32)


def _s5_kernel(u_ref, bbd_ref, tab_ref, cbd_ref, d_ref, wglu_ref, bglu_ref, o_ref,
               bu_scr, x_scr, carry_scr, *, tile):
    ns = S5_GROUPS * S5_STATE
    lc = 512

    @pl.when(pl.program_id(1) == 0)
    def _():
        carry_scr[...] = jnp.zeros_like(carry_scr)

    u = u_ref[0]
    bu_scr[...] = jnp.dot(u, bbd_ref[...], preferred_element_type=F32)

    def block(r, c):
        off = pl.multiple_of(r * S5_ROWS, S5_ROWS)
        for ch in range(ns // lc):
            re_l = slice(ch * lc, (ch + 1) * lc)
            im_l = slice(ns + ch * lc, ns + (ch + 1) * lc)
            xr = bu_scr[pl.ds(off, S5_ROWS), re_l]
            xi = bu_scr[pl.ds(off, S5_ROWS), im_l]
            for k, d in enumerate((1, 2, 4)):
                sr = pltpu.roll(xr, d, 0)
                si = pltpu.roll(xi, d, 0)
                mr = tab_ref[2 * k, :, re_l]
                mi = tab_ref[2 * k + 1, :, re_l]
                xr, xi = xr + (mr * sr - mi * si), xi + (mr * si + mi * sr)
            ar = tab_ref[6, :, re_l]
            ai = tab_ref[7, :, re_l]
            cr = carry_scr[:, re_l]
            ci = carry_scr[:, im_l]
            xr, xi = xr + (ar * cr - ai * ci), xi + (ar * ci + ai * cr)
            x_scr[pl.ds(off, S5_ROWS), re_l] = xr
            x_scr[pl.ds(off, S5_ROWS), im_l] = xi
            carry_scr[:, re_l] = jnp.broadcast_to(xr[S5_ROWS - 1:S5_ROWS], (S5_ROWS, lc))
            carry_scr[:, im_l] = jnp.broadcast_to(xi[S5_ROWS - 1:S5_ROWS], (S5_ROWS, lc))
        return c

    lax.fori_loop(0, tile // S5_ROWS, block, 0)
    y = jnp.dot(x_scr[...].astype(BF16), cbd_ref[...], preferred_element_type=F32)
    y = _gelu(y + d_ref[...] * u.astype(F32))
    gate = _sigmoid(jnp.dot(y.astype(BF16), wglu_ref[...], preferred_element_type=F32) + bglu_ref[...])
    o_ref[0] = (y * gate).astype(o_ref.dtype)


def _s5(proj3, bbd, cbd, tabs, d_skip, w_glu, b_glu):
    bsz, seq, _ = proj3.shape
    bw = BRANCH_WIDTH
    ns = S5_GROUPS * S5_STATE
    tile = min(256, seq)
    const = lambda shape: pl.BlockSpec(shape, lambda b, i: (0,) * len(shape))
    return pl.pallas_call(
        functools.partial(_s5_kernel, tile=tile),
        grid=(bsz, seq // tile),
        in_specs=[pl.BlockSpec((1, tile, bw), lambda b, i: (b, i, GATE_COLS // bw)),
                  const((bw, 2 * ns)), const((8, S5_ROWS, ns)), const((2 * ns, bw)),
                  const((1, bw)), const((bw, bw)), const((1, bw))],
        out_specs=pl.BlockSpec((1, tile, bw), lambda b, i: (b, i, 0)),
        out_shape=jax.ShapeDtypeStruct((bsz, seq, bw), BF16),
        scratch_shapes=[pltpu.VMEM((tile, 2 * ns), F32), pltpu.VMEM((tile, 2 * ns), F32),
                        pltpu.VMEM((S5_ROWS, 2 * ns), F32)],
        compiler_params=_cparams(("parallel", "arbitrary")),
        name="s5",
    )(proj3, bbd, tabs, cbd, d_skip.astype(F32).reshape(1, bw), w_glu.astype(BF16),
      b_glu.astype(F32).reshape(1, bw))


def _hgrn_kernel(q_ref, f_ref, i_ref, g_ref, lb_ref, gain_ref, tril_ref, o_ref, st_scr, *, seq):
    c, sub = HG_CHUNK, HG_SUB
    lb = lb_ref[0]
    gain = gain_ref[0]
    tril = tril_ref[...]
    row = lax.broadcasted_iota(jnp.int32, (c, LANE), 0)
    rsub = jnp.bitwise_and(row, sub - 1)
    tcol = lax.broadcasted_iota(jnp.int32, (sub, c), 1)
    st_scr[...] = jnp.zeros_like(st_scr)
    nt = (((1,), (1,)), ((), ()))

    def chunk(ci, carry):
        off = pl.multiple_of(ci * c, c)
        q = _silu(q_ref[0, pl.ds(off, c), :].astype(F32))
        fg = lb + (1.0 - lb) * _sigmoid(f_ref[0, pl.ds(off, c), :].astype(F32))
        logf = jnp.log(jnp.maximum(fg, HG_F_MIN))
        k = 1.0 - fg
        v = i_ref[0, pl.ds(off, c), :]
        vf = v.astype(F32)
        bc = _dot01_left(tril, logf)
        o = jnp.zeros((c, LANE), F32)
        for lag in range(sub):
            if lag == 0:
                kd, bd, vd = k, bc, vf
            else:
                kd = pltpu.roll(k, lag, 0)
                bd = pltpu.roll(bc, lag, 0)
                vd = pltpu.roll(vf, lag, 0)
            valid = rsub >= lag
            e = jnp.exp(jnp.where(valid, bc - bd, 0.0))
            w = jnp.sum(q * kd * e, axis=-1, keepdims=True)
            o = o + jnp.where(valid, w * vd, 0.0)
        parts = []
        for j in range(c // sub - 1):
            lo, hi = sub * j, sub * (j + 1)
            bend = bc[hi - 1:hi, :]
            qj = (q * jnp.exp(jnp.minimum(bc - bend, 0.0))).astype(BF16)
            kj = (k[lo:hi] * jnp.exp(bend - bc[lo:hi])).astype(BF16)
            st = lax.dot_general(kj, qj, nt, preferred_element_type=F32)
            parts.append(jnp.where(tcol >= hi, st, 0.0))
        parts.append(jnp.zeros((sub, c), F32))
        scores = jnp.concatenate(parts, axis=0).T
        o = o + jnp.dot(scores.astype(BF16), v, preferred_element_type=F32)
        st_prev = st_scr[...]
        qe = (q * jnp.exp(bc)).astype(BF16)
        o = o + lax.dot_general(qe, st_prev.astype(BF16), nt, preferred_element_type=F32)
        blast = bc[c - 1:c, :]
        kdec = (k * jnp.exp(blast - bc)).astype(BF16)
        st_scr[...] = st_prev * jnp.exp(blast) + jnp.dot(vf.T.astype(BF16), kdec,
                                                         preferred_element_type=F32)
        o = o * lax.rsqrt(jnp.mean(o * o, axis=-1, keepdims=True) + EPS) * gain
        o = o * _silu(g_ref[0, pl.ds(off, c), :].astype(F32))
        o_ref[0, pl.ds(off, c), :] = o.astype(o_ref.dtype)
        return carry

    lax.fori_loop(0, seq // c, chunk, 0)


def _hgrn(proj3, lower_bound, norm_gain):
    bsz, seq, _ = proj3.shape
    c = HG_CHUNK
    tril = jnp.tril(jnp.ones((c, c), F32)).astype(BF16)
    base = PROJ_BLK0 + BRANCH_WIDTH // LANE

    def col(k):
        return pl.BlockSpec((1, seq, LANE), lambda b, h: (b, 0, base + k * HG_HEADS + h))

    vec = pl.BlockSpec((1, 1, LANE), lambda b, h: (h, 0, 0))
    return pl.pallas_call(
        functools.partial(_hgrn_kernel, seq=seq),
        grid=(bsz, HG_HEADS),
        in_specs=[col(0), col(1), col(2), col(3), vec, vec,
                  pl.BlockSpec((c, c), lambda b, h: (0, 0))],
        out_specs=pl.BlockSpec((1, seq, LANE), lambda b, h: (b, 0, h)),
        out_shape=jax.ShapeDtypeStruct((bsz, seq, BRANCH_WIDTH), BF16),
        scratch_shapes=[pltpu.VMEM((LANE, LANE), F32)],
        compiler_params=_cparams(("parallel", "parallel")),
        name="hgrn2",
    )(proj3, proj3, proj3, proj3, lower_bound.reshape(HG_HEADS, 1, LANE),
      norm_gain.reshape(HG_HEADS, 1, LANE), tril)


def _sb_kernel(q_ref, k_ref, v_ref, up_ref, o_ref, acc_scr, later_scr):
    qi = pl.program_id(1)
    t = SB_TILE
    up = up_ref[...]
    scale = 1.0 / math.sqrt(SB_DIM)
    row = lax.broadcasted_iota(jnp.int32, (t, t), 0)
    colv = lax.broadcasted_iota(jnp.int32, (t, t), 1)
    below = colv < row
    nt = (((1,), (1,)), ((), ()))
    acc_scr[...] = jnp.zeros_like(acc_scr)
    later_scr[...] = jnp.zeros_like(later_scr)

    def body(jj, carry):
        off = pl.multiple_of((qi - jj) * t, t)
        mask = jnp.logical_or(jj > 0, below)
        heads = [slice(h * SB_DIM, (h + 1) * SB_DIM) for h in range(SB_HEADS)]
        zs = [lax.dot_general(q_ref[0, :, hl], k_ref[0, pl.ds(off, t), hl], nt,
                              preferred_element_type=F32) * scale for hl in heads]
        logits, suffixes, totals = [], [], []
        for h in range(SB_HEADS):
            z = zs[h]
            sp = jnp.maximum(z, 0.0) + jnp.log(1.0 + jnp.exp(-jnp.abs(z)))
            log_not = jnp.where(mask, -sp, 0.0)
            suffixes.append(_dot01_right2(log_not, up))
            logits.append(z - sp)
            totals.append(jnp.sum(log_not, axis=-1, keepdims=True))
        for h in range(SB_HEADS):
            later = later_scr[h]
            w = jnp.where(mask, jnp.exp(logits[h] + suffixes[h] + later), 0.0)
            acc_scr[h] += jnp.dot(w.astype(BF16), v_ref[0, pl.ds(off, t), heads[h]],
                                  preferred_element_type=F32)
            later_scr[h] = later + totals[h]
        return carry

    lax.fori_loop(0, qi + 1, body, 0)
    for h in range(SB_HEADS):
        o_ref[0, :, h * SB_DIM:(h + 1) * SB_DIM] = acc_scr[h].astype(o_ref.dtype)


def _stick_breaking(proj3):
    bsz, seq, _ = proj3.shape
    t = SB_TILE
    bw = BRANCH_WIDTH
    base = (GATE_COLS + 5 * bw) // bw
    up = (jnp.arange(t)[:, None] > jnp.arange(t)[None, :]).astype(BF16)
    return pl.pallas_call(
        _sb_kernel,
        grid=(bsz, seq // t),
        in_specs=[pl.BlockSpec((1, t, bw), lambda b, i: (b, i, base)),
                  pl.BlockSpec((1, seq, bw), lambda b, i: (b, 0, base + 1)),
                  pl.BlockSpec((1, seq, bw), lambda b, i: (b, 0, base + 2)),
                  pl.BlockSpec((t, t), lambda b, i: (0, 0))],
        out_specs=pl.BlockSpec((1, t, bw), lambda b, i: (b, i, 0)),
        out_shape=jax.ShapeDtypeStruct((bsz, seq, bw), BF16),
        scratch_shapes=[pltpu.VMEM((SB_HEADS, t, SB_DIM), F32), pltpu.VMEM((SB_HEADS, t, 1), F32)],
        compiler_params=_cparams(("parallel", "arbitrary")),
        name="stick_breaking",
    )(proj3, proj3, proj3, up)


def _pool_kernel(p_ref, bc_ref, bp_ref, w_ref, sc_ref, o_ref, *, seq):
    g = pl.program_id(1)
    tile = LANE
    window = lax.shift_left(jnp.int32(2), g)
    row = lax.broadcasted_iota(jnp.int32, (tile, LANE), 0)
    band_cur = bc_ref[0]
    band_prev = bp_ref[0]
    wmix = w_ref[0]
    scale = sc_ref[0]

    group = min(4, seq // tile)

    def body(it, carry):
        offs, curs, wins = [], [], []
        for k in range(group):
            i = it * group + k
            off = pl.multiple_of(i * tile, tile)
            poff = pl.multiple_of(jnp.maximum(i - 1, 0) * tile, tile)
            cur = p_ref[0, pl.ds(off, tile), :]
            prev = p_ref[0, pl.ds(poff, tile), :]
            win = jnp.dot(band_cur, cur, preferred_element_type=F32)
            win = win + jnp.where(i > 0, jnp.dot(band_prev, prev, preferred_element_type=F32), 0.0)
            offs.append(off)
            curs.append(cur)
            wins.append(win)
        for k in range(group):
            count = jnp.minimum(row + (offs[k] + 1), window).astype(F32)
            pooled = wins[k] / count - curs[k].astype(F32)
            mixed = jnp.dot(pooled.astype(BF16), wmix, preferred_element_type=F32) * scale
            o_ref[0, pl.ds(offs[k], tile), :] = mixed.astype(o_ref.dtype)
        return carry

    lax.fori_loop(0, seq // (tile * group), body, 0)


def _pool(proj3, pool_w, pool_scale):
    bsz, seq, _ = proj3.shape
    ng = len(POOL_WINDOWS)
    base = PROJ_BLK0 + 8 * BRANCH_WIDTH // LANE
    t = jnp.arange(LANE)
    lag = t[:, None] - t[None, :]
    wins = jnp.asarray(POOL_WINDOWS)[:, None, None]
    band_cur = ((lag[None] >= 0) & (lag[None] < wins)).astype(BF16)
    band_prev = (((lag[None] + LANE) >= 0) & ((lag[None] + LANE) < wins)).astype(BF16)
    mat = pl.BlockSpec((1, LANE, LANE), lambda b, g: (g, 0, 0))
    return pl.pallas_call(
        functools.partial(_pool_kernel, seq=seq),
        grid=(bsz, ng),
        in_specs=[pl.BlockSpec((1, seq, LANE), lambda b, g: (b, 0, base + g)),
                  mat, mat, mat,
                  pl.BlockSpec((1, 1, LANE), lambda b, g: (g, 0, 0))],
        out_specs=pl.BlockSpec((1, seq, LANE), lambda b, g: (b, 0, g)),
        out_shape=jax.ShapeDtypeStruct((bsz, seq, BRANCH_WIDTH), BF16),
        compiler_params=_cparams(("parallel", "parallel")),
        name="pool",
    )(proj3, band_cur, band_prev, pool_w.astype(BF16), pool_scale.astype(F32).reshape(ng, 1, LANE))


def _merge_kernel(x_ref, g1_ref, s5_ref, hg_ref, sb_ref, po_ref, gt0_ref, gt1_ref, gt2_ref, gt3_ref,
                  wbr_ref, wout_ref, o_ref):
    dot = functools.partial(jnp.dot, preferred_element_type=F32)
    branches = (s5_ref, hg_ref, sb_ref, po_ref)
    gates = (gt0_ref, gt1_ref, gt2_ref, gt3_ref)
    merged = None
    for n in range(N_BRANCH):
        term = gates[n][...].astype(F32) * dot(branches[n][...], wbr_ref[n])
        merged = term if merged is None else merged + term
    mix = dot(merged.astype(BF16), wout_ref[...])
    o_ref[...] = x_ref[...] + g1_ref[0] * mix


def _merge(x2d, g1, s5o, hg, sb, po, proj, w_branch, w_out, seq):
    t, d = x2d.shape
    bw = BRANCH_WIDTH
    tm = min(256, seq)
    per_b = seq // tm
    tok = lambda w: pl.BlockSpec((tm, w), lambda i: (i, 0))
    gate = lambda n: pl.BlockSpec((tm, d), lambda i, n=n: (i, n))
    const = lambda shape: pl.BlockSpec(shape, lambda i: (0,) * len(shape))
    return pl.pallas_call(
        _merge_kernel,
        grid=(t // tm,),
        in_specs=[tok(d), pl.BlockSpec((1, 1, d), lambda i: (i // per_b, 0, 0)),
                  tok(bw), tok(bw), tok(bw), tok(bw),
                  gate(0), gate(1), gate(2), gate(3),
                  const((N_BRANCH, bw, d)), const((d, d))],
        out_specs=tok(d),
        out_shape=jax.ShapeDtypeStruct((t, d), F32),
        compiler_params=_cparams(("parallel",)),
        name="merge",
    )(x2d, g1, s5o, hg, sb, po, proj, proj, proj, proj, w_branch.astype(BF16), w_out.astype(BF16))


def _top_values(s, n):
    vals = []
    for r in range(n):
        cur = s if r == 0 else jnp.where(s < vals[-1], s, NEG_BIG)
        vals.append(jnp.max(cur, axis=0, keepdims=True))
    return vals


def _prefix_length(rows, pred):
    c1 = pred(rows[7])
    c2 = pred(jnp.where(c1, rows[11], rows[3]))
    c3 = pred(jnp.where(c1, jnp.where(c2, rows[13], rows[9]), jnp.where(c2, rows[5], rows[1])))
    pick = lambda a: jnp.where(c3, rows[a + 2], rows[a])
    c4 = pred(jnp.where(c1, jnp.where(c2, pick(12), pick(8)), jnp.where(c2, pick(4), pick(0))))
    c5 = pred(rows[15])
    one = lambda c, v: jnp.where(c, v, 0.0)
    return (one(c1, 8.0) + one(c2, 4.0)) + (one(c3, 2.0) + one(c4, 1.0)) + one(c5, 1.0)


def _route_kernel(x_ref, gain_ref, sh_ref, sc_ref, wq_ref, keys_ref,
                  ht_ref, cnt_ref, e1_ref, rank_ref, e2_ref):
    k = PEER_TOPK
    h = _rms_modulate(x_ref[...], gain_ref[...], sh_ref[0], sc_ref[0])
    ht = h.T.astype(BF16)
    ht_ref[...] = ht
    qt = jnp.dot(wq_ref[...], ht, preferred_element_type=F32)
    for hd in range(PEER_HEADS):
        sc = []
        for half in range(2):
            lo = (2 * hd + half) * PEER_NKEYS
            qs = qt[lo:lo + PEER_NKEYS, :].astype(BF16)
            sc.append(jnp.dot(keys_ref[hd, half], qs, preferred_element_type=F32))
        top1 = _top_values(sc[0], k)
        top2 = _top_values(sc[1], k)
        v1 = jnp.concatenate(top1, axis=0)
        v2 = jnp.concatenate(top2, axis=0)
        cand = [top1[0] + v2] + [top1[a] + v2[:8] for a in range(1, 8)] + [v1[8:] + top2[0]]
        best = _top_values(jnp.concatenate(cand, axis=0), k)
        tau = best[k - 1]
        z = None
        for r in range(k):
            e = jnp.exp(best[r] - best[0])
            z = e if z is None else z + e
        s1, s2 = sc
        cnt_ref[hd] = _prefix_length(top2, lambda row: s1 + row >= tau)
        rank_ref[hd] = _prefix_length(top2, lambda row: row > s2).astype(BF16)
        e1_ref[hd] = jnp.exp(s1 - top1[0]) / z
        e2_ref[hd] = jnp.exp(s2 - top2[0]).astype(BF16)


def _route(x2d, gain, shift, scale, wq_t, keys, seq):
    t, d = x2d.shape
    tm = min(256, seq)
    per_b = seq // tm
    nh, nk = PEER_HEADS, PEER_NKEYS
    big = pl.BlockSpec((nh, nk, tm), lambda i: (0, 0, i))
    big_f32 = jax.ShapeDtypeStruct((nh, nk, t), F32)
    big_bf16 = jax.ShapeDtypeStruct((nh, nk, t), BF16)
    return pl.pallas_call(
        _route_kernel,
        grid=(t // tm,),
        in_specs=[pl.BlockSpec((tm, d), lambda i: (i, 0)),
                  pl.BlockSpec((1, d), lambda i: (0, 0)),
                  pl.BlockSpec((1, 1, d), lambda i: (i // per_b, 0, 0)),
                  pl.BlockSpec((1, 1, d), lambda i: (i // per_b, 0, 0)),
                  pl.BlockSpec(wq_t.shape, lambda i: (0, 0)),
                  pl.BlockSpec(keys.shape, lambda i: (0, 0, 0, 0))],
        out_specs=[pl.BlockSpec((d, tm), lambda i: (0, i)), big, big, big, big],
        out_shape=[jax.ShapeDtypeStruct((d, t), BF16), big_f32, big_f32, big_bf16, big_bf16],
        compiler_params=_cparams(("parallel",)),
        name="peer_route",
    )(x2d, gain.reshape(1, d), shift, scale, wq_t, keys)


def _rows_bf16(row, n):
    tile = jnp.broadcast_to(row, (16, row.shape[1])).astype(BF16)
    return jnp.concatenate([tile] * (n // 16), axis=0)


def _peer_kernel(ht_ref, u_ref, vt_ref, cnt_ref, e1_ref, rank_ref, e2_ref, x_ref, g2_ref, fg_ref,
                 o_ref, acc_ref, *, te, tm, final_norm):
    e = pl.program_id(1)
    nk = PEER_NKEYS
    blk = PEER_BLOCK
    rows = blk // nk
    dot = functools.partial(jnp.dot, preferred_element_type=F32)

    @pl.when(e == 0)
    def _():
        acc_ref[...] = jnp.zeros_like(acc_ref)

    nblk = te // blk

    def first(c):
        return dot(u_ref[c * blk:(c + 1) * blk, :], ht_ref[...])

    def gated(c, act):
        parts = []
        for ii in range(rows):
            i = (e * nblk + c) * rows + ii
            gate = jnp.zeros((nk, tm), BF16)
            for hd in range(PEER_HEADS):
                count = _rows_bf16(cnt_ref[hd, pl.ds(i, 1), :], nk)
                e1 = _rows_bf16(e1_ref[hd, pl.ds(i, 1), :], nk)
                gate = gate + jnp.where(rank_ref[hd] < count, e1 * e2_ref[hd], jnp.zeros((), BF16))
            parts.append(gate * _gelu(act[ii * nk:(ii + 1) * nk, :]).astype(BF16))
        return jnp.concatenate(parts, axis=0)

    ahead = 2
    acts = {c: first(c) for c in range(min(ahead, nblk))}
    for c in range(nblk):
        p = gated(c, acts.pop(c))
        if c + ahead < nblk:
            acts[c + ahead] = first(c + ahead)
        acc_ref[...] += dot(vt_ref[:, c * blk:(c + 1) * blk], p)

    @pl.when(e == pl.num_programs(1) - 1)
    def _():
        y = x_ref[...] + g2_ref[0] * acc_ref[...].T
        if final_norm:
            y = y * lax.rsqrt(jnp.mean(y * y, axis=-1, keepdims=True) + EPS) * fg_ref[...]
        o_ref[...] = y


def _peer(ht, u_bf, v_bf, cnt, e1, rank2, e2, x2d, g2, final_gain, final_norm, seq):
    t, d = x2d.shape
    ne = u_bf.shape[0]
    tm = min(512, seq)
    te = 1024
    per_b = seq // tm
    nh, nk = PEER_HEADS, PEER_NKEYS
    big = pl.BlockSpec((nh, nk, tm), lambda i, g: (0, 0, i))
    return pl.pallas_call(
        functools.partial(_peer_kernel, te=te, tm=tm, final_norm=final_norm),
        grid=(t // tm, ne // te),
        in_specs=[pl.BlockSpec((d, tm), lambda i, g: (0, i)),
                  pl.BlockSpec((te, d), lambda i, g: (g, 0)),
                  pl.BlockSpec((d, te), lambda i, g: (0, g)),
                  big, big, big, big,
                  pl.BlockSpec((tm, d), lambda i, g: (i, 0), pipeline_mode=pl.Buffered(1)),
                  pl.BlockSpec((1, 1, d), lambda i, g: (i // per_b, 0, 0)),
                  pl.BlockSpec((1, d), lambda i, g: (0, 0))],
        out_specs=pl.BlockSpec((tm, d), lambda i, g: (i, 0)),
        out_shape=jax.ShapeDtypeStruct((t, d), F32),
        scratch_shapes=[pltpu.VMEM((d, tm), F32)],
        compiler_params=_cparams(("parallel", "arbitrary"), vmem=PEER_VMEM_LIMIT),
        name="peer_experts",
    )(ht, u_bf, v_bf.T, cnt, e1, rank2, e2, x2d, g2, final_gain.astype(F32).reshape(1, d))


def _layer(x2d, c, bsz, seq, ada_w, ada_b, norm_mix_gain, norm_ffn_gain, w_in, s5_params, s5_d,
           s5_w_glu, s5_b_glu, lower_bound, hg_norm_gain, pool_w, pool_scale, w_gate, w_branch, w_out,
           peer_w_query, peer_sub_keys, peer_u, peer_v, final_gain, last):
    d = D_MODEL
    mod = _ada(c.astype(F32), ada_w.astype(F32), ada_b.astype(F32))
    sh1, sc1, g1, sh2, sc2, g2 = [m.reshape(bsz, 1, d) for m in jnp.split(mod, 6, axis=-1)]
    wcat = jnp.concatenate([w_gate[n] for n in range(N_BRANCH)] + [w_in], axis=1).astype(BF16)
    proj = _inproj(x2d, norm_mix_gain.astype(F32), sh1, sc1, wcat, seq)
    proj3 = proj.reshape(bsz, seq, CAT_COLS)
    bbd, cbd, tabs = _s5_tables(*s5_params)
    s5o = _s5(proj3, bbd, cbd, tabs, s5_d, s5_w_glu, s5_b_glu)
    hg = _hgrn(proj3, lower_bound, hg_norm_gain.astype(F32))
    sb = _stick_breaking(proj3)
    po = _pool(proj3, pool_w, pool_scale)
    t = bsz * seq
    flat = lambda a: a.reshape(t, BRANCH_WIDTH)
    x1 = _merge(x2d, g1, flat(s5o), flat(hg), flat(sb), flat(po), proj, w_branch, w_out, seq)
    ht, cnt, e1, rank2, e2 = _route(x1, norm_ffn_gain.astype(F32), sh2, sc2,
                                    peer_w_query.T.astype(BF16), peer_sub_keys.astype(BF16), seq)
    return _peer(ht, peer_u.astype(BF16), peer_v.astype(BF16), cnt, e1, rank2, e2, x1, g2, final_gain, last,
                 seq)


def kernel(x, c, ada_w, ada_b, norm_mix_gain, norm_ffn_gain, w_in, s5_lambda_re, s5_lambda_im, s5_log_step, s5_b_re, s5_b_im, s5_c_re, s5_c_im, s5_d, s5_w_glu, s5_b_glu, hg_lb_logits, hg_norm_gain, pool_w, pool_scale, w_gate, w_branch, w_out, peer_w_query, peer_sub_keys, peer_u, peer_v, final_gain):
    bsz, seq, d = x.shape
    depth = ada_w.shape[0]
    lb_soft = jax.nn.softmax(hg_lb_logits.astype(F32), axis=0)
    lower_bounds = jnp.cumsum(lb_soft, axis=0) - lb_soft[0:1]
    x2d = x.astype(F32).reshape(bsz * seq, d)
    for l in range(depth):
        s5_params = (s5_lambda_re[l], s5_lambda_im[l], s5_log_step[l], s5_b_re[l], s5_b_im[l],
                     s5_c_re[l], s5_c_im[l])
        x2d = _layer(x2d, c, bsz, seq, ada_w[l], ada_b[l], norm_mix_gain[l], norm_ffn_gain[l], w_in[l],
                     s5_params, s5_d[l], s5_w_glu[l], s5_b_glu[l], lower_bounds[l], hg_norm_gain[l],
                     pool_w[l], pool_scale[l], w_gate[l], w_branch[l], w_out[l], peer_w_query[l],
                     peer_sub_keys[l], peer_u[l], peer_v[l], final_gain, l == depth - 1)
    return x2d.reshape(bsz, seq, d).astype(x.dtype)
```

```python
import functools
import math

import jax
import jax.numpy as jnp
from jax import lax
from jax.experimental import pallas as pl
from jax.experimental.pallas import tpu as pltpu

F32 = jnp.float32
BF16 = jnp.bfloat16

D_MODEL = 2048
BRANCH_WIDTH = 512
N_BRANCH = 4
LANE = 128
S5_GROUP = 16
S5_GROUPS = 32
S5_STATE = 64
S5_MIN_NEG = 1e-4
S5_ROWS = 8
HG_HEADS = 4
HG_F_MIN = 1e-6
HG_CHUNK = 128
HG_SUB = 16
SB_HEADS = 4
SB_DIM = 128
SB_TILE = 256
POOL_WINDOWS = (2, 4, 8, 16)
PEER_HEADS = 8
PEER_NKEYS = 128
PEER_TOPK = 16
PEER_BLOCK = 256
EPS = 1e-6
NEG_BIG = -3.0e38

GATE_COLS = N_BRANCH * D_MODEL
IN_COLS = 9 * BRANCH_WIDTH
CAT_COLS = GATE_COLS + IN_COLS
PROJ_BLK0 = GATE_COLS // LANE
INPROJ_CHUNK = 256
VMEM_LIMIT = 48 * 1024 * 1024
PEER_VMEM_LIMIT = 56 * 1024 * 1024


def _cparams(semantics, vmem=VMEM_LIMIT):
    return pltpu.CompilerParams(dimension_semantics=semantics, vmem_limit_bytes=vmem)


def _split3(x):
    hi = x.astype(BF16)
    r1 = x - hi.astype(F32)
    mid = r1.astype(BF16)
    lo = (r1 - mid.astype(F32)).astype(BF16)
    return hi, mid, lo


def _dot01_left(m01, x):
    hi, mid, lo = _split3(x)
    d = functools.partial(jnp.dot, preferred_element_type=F32)
    return d(m01, hi) + d(m01, mid) + d(m01, lo)


def _dot01_right2(x, m01):
    hi = x.astype(BF16)
    lo = (x - hi.astype(F32)).astype(BF16)
    d = functools.partial(jnp.dot, preferred_element_type=F32)
    return d(hi, m01) + d(lo, m01)


def _dot_f32(a, b):
    ah, am, al = _split3(a)
    bh, bm, bl = _split3(b)
    d = functools.partial(jnp.dot, preferred_element_type=F32)
    return (d(ah, bh) + (d(ah, bm) + d(am, bh))) + ((d(ah, bl) + d(al, bh)) + d(am, bm))


def _gelu(x):
    return 0.5 * x * (1.0 + lax.erf(x * (1.0 / math.sqrt(2.0))))


def _sigmoid(x):
    return 0.5 * jnp.tanh(0.5 * x) + 0.5


def _silu(x):
    return x * _sigmoid(x)


def _rms_modulate(x, gain, shift, scale):
    y = x * lax.rsqrt(jnp.mean(x * x, axis=-1, keepdims=True) + EPS) * gain
    return y * (1.0 + scale) + shift


def _ada_kernel(c_ref, w_ref, b_ref, o_ref):
    o_ref[...] = _dot_f32(_silu(c_ref[...]), w_ref[...]) + b_ref[...]


def _ada(c, w, b):
    bsz, d = c.shape
    n = w.shape[1]
    tn = 1024
    return pl.pallas_call(
        _ada_kernel,
        grid=(n // tn,),
        in_specs=[pl.BlockSpec((bsz, d), lambda j: (0, 0)),
                  pl.BlockSpec((d, tn), lambda j: (0, j)),
                  pl.BlockSpec((1, tn), lambda j: (0, j))],
        out_specs=pl.BlockSpec((bsz, tn), lambda j: (0, j)),
        out_shape=jax.ShapeDtypeStruct((bsz, n), F32),
        compiler_params=_cparams(("arbitrary",)),
        name="ada_mod",
    )(c, w, b.reshape(1, n))


def _inproj_kernel(x_ref, gain_ref, sh_ref, sc_ref, w_ref, o_ref, h_ref, *, gate_chunks):
    j = pl.program_id(1)

    @pl.when(j == 0)
    def _():
        def rows(r, c):
            sl = pl.ds(pl.multiple_of(r * LANE, LANE), LANE)
            h_ref[sl, :] = _rms_modulate(x_ref[sl, :], gain_ref[...], sh_ref[0], sc_ref[0]).astype(BF16)
            return c

        lax.fori_loop(0, x_ref.shape[0] // LANE, rows, 0)

    cw = INPROJ_CHUNK
    nch = w_ref.shape[1] // cw
    mm = lambda k: jnp.dot(h_ref[...], w_ref[:, k * cw:(k + 1) * cw], preferred_element_type=F32)
    parts = {0: mm(0), 1: mm(1)}
    for k in range(nch):
        r = parts.pop(k)
        if k + 2 < nch:
            parts[k + 2] = mm(k + 2)
        is_gate = j * nch + k < gate_chunks
        o_ref[:, k * cw:(k + 1) * cw] = jnp.where(is_gate, _sigmoid(r), r).astype(o_ref.dtype)


def _inproj(x2d, gain, shift, scale, wcat, seq):
    t, d = x2d.shape
    n = wcat.shape[1]
    tm = min(1024, seq)
    tn = 5 * INPROJ_CHUNK
    per_b = seq // tm
    return pl.pallas_call(
        functools.partial(_inproj_kernel, gate_chunks=GATE_COLS // INPROJ_CHUNK),
        grid=(t // tm, n // tn),
        in_specs=[pl.BlockSpec((tm, d), lambda i, j: (i, 0)),
                  pl.BlockSpec((1, d), lambda i, j: (0, 0)),
                  pl.BlockSpec((1, 1, d), lambda i, j: (i // per_b, 0, 0)),
                  pl.BlockSpec((1, 1, d), lambda i, j: (i // per_b, 0, 0)),
                  pl.BlockSpec((d, tn), lambda i, j: (0, j))],
        out_specs=pl.BlockSpec((tm, tn), lambda i, j: (i, j)),
        out_shape=jax.ShapeDtypeStruct((t, n), BF16),
        scratch_shapes=[pltpu.VMEM((tm, d), BF16)],
        compiler_params=_cparams(("parallel", "arbitrary")),
        name="inproj",
    )(x2d, gain.reshape(1, d), shift, scale, wcat)


def _s5_tables(lam_re, lam_im, log_step, b_re, b_im, c_re, c_im):
    g, p, h = S5_GROUPS, S5_STATE, S5_GROUP
    lr = jnp.minimum(lam_re.astype(F32), -S5_MIN_NEG)
    li = lam_im.astype(F32)
    step = jnp.exp(log_step.astype(F32))[:, None]
    mag = jnp.exp(lr * step)
    ab_re = mag * jnp.cos(li * step)
    ab_im = mag * jnp.sin(li * step)
    den = lr * lr + li * li
    nr = ab_re - 1.0
    fr = (nr * lr + ab_im * li) / den
    fi = (ab_im * lr - nr * li) / den
    br, bi = b_re.astype(F32), b_im.astype(F32)
    bb_re = fr[..., None] * br - fi[..., None] * bi
    bb_im = fr[..., None] * bi + fi[..., None] * br
    eye_g = jnp.eye(g, dtype=F32)

    def in_map(m):
        return jnp.einsum('gph,gk->ghkp', m, eye_g).reshape(g * h, g * p)

    def out_map(m):
        return jnp.einsum('ghp,gk->gpkh', m, eye_g).reshape(g * p, g * h)

    bbd = jnp.concatenate([in_map(bb_re), in_map(bb_im)], axis=1).astype(BF16)
    cbd = jnp.concatenate([out_map(c_re.astype(F32)), out_map(-c_im.astype(F32))], axis=0).astype(BF16)
    prs, pis = [ab_re], [ab_im]
    for _ in range(S5_ROWS - 1):
        pr_, pi_ = prs[-1], pis[-1]
        prs.append(pr_ * ab_re - pi_ * ab_im)
        pis.append(pr_ * ab_im + pi_ * ab_re)
    rows = jnp.arange(S5_ROWS)[:, None]
    tabs = []
    for d in (1, 2, 4):
        tabs.append(jnp.where(rows >= d, prs[d - 1].reshape(1, g * p), 0.0))
        tabs.append(jnp.where(rows >= d, pis[d - 1].reshape(1, g * p), 0.0))
    tabs.append(jnp.stack(prs).reshape(S5_ROWS, g * p))
    tabs.append(jnp.stack(pis).reshape(S5_ROWS, g * p))
    return bbd, cbd, jnp.stack(tabs).astype(F32)


def _s5_kernel(u_ref, bbd_ref, tab_ref, cbd_ref, d_ref, wglu_ref, bglu_ref, o_ref,
               bu_scr, x_scr, carry_scr, *, tile):
    ns = S5_GROUPS * S5_STATE
    lc = 512

    @pl.when(pl.program_id(1) == 0)
    def _():
        carry_scr[...] = jnp.zeros_like(carry_scr)

    u = u_ref[0]
    dot = functools.partial(jnp.dot, preferred_element_type=F32)
    gl = lc // (S5_STATE // S5_GROUP)
    for ch in range(ns // lc):
        ub = u[:, ch * gl:(ch + 1) * gl]
        for part in (0, ns):
            cols = slice(part + ch * lc, part + (ch + 1) * lc)
            bu_scr[:, cols] = dot(ub, bbd_ref[ch * gl:(ch + 1) * gl, cols])

    def block(r, c):
        off = pl.multiple_of(r * S5_ROWS, S5_ROWS)
        for ch in range(ns // lc):
            re_l = slice(ch * lc, (ch + 1) * lc)
            im_l = slice(ns + ch * lc, ns + (ch + 1) * lc)
            xr = bu_scr[pl.ds(off, S5_ROWS), re_l]
            xi = bu_scr[pl.ds(off, S5_ROWS), im_l]
            for k, d in enumerate((1, 2, 4)):
                sr = pltpu.roll(xr, d, 0)
                si = pltpu.roll(xi, d, 0)
                mr = tab_ref[2 * k, :, re_l]
                mi = tab_ref[2 * k + 1, :, re_l]
                xr, xi = xr + (mr * sr - mi * si), xi + (mr * si + mi * sr)
            ar = tab_ref[6, :, re_l]
            ai = tab_ref[7, :, re_l]
            cr = carry_scr[:, re_l]
            ci = carry_scr[:, im_l]
            xr, xi = xr + (ar * cr - ai * ci), xi + (ar * ci + ai * cr)
            x_scr[pl.ds(off, S5_ROWS), re_l] = xr
            x_scr[pl.ds(off, S5_ROWS), im_l] = xi
            carry_scr[:, re_l] = jnp.broadcast_to(xr[S5_ROWS - 1:S5_ROWS], (S5_ROWS, lc))
            carry_scr[:, im_l] = jnp.broadcast_to(xi[S5_ROWS - 1:S5_ROWS], (S5_ROWS, lc))
        return c

    lax.fori_loop(0, tile // S5_ROWS, block, 0)
    ys = []
    for ch in range(ns // lc):
        out_l = slice(ch * gl, (ch + 1) * gl)
        ys.append(sum(dot(x_scr[:, part + ch * lc:part + (ch + 1) * lc].astype(BF16),
                          cbd_ref[part + ch * lc:part + (ch + 1) * lc, out_l]) for part in (0, ns)))
    y = _gelu(jnp.concatenate(ys, axis=1) + d_ref[...] * u.astype(F32))
    gate = _sigmoid(dot(y.astype(BF16), wglu_ref[...]) + bglu_ref[...])
    o_ref[0] = (y * gate).astype(o_ref.dtype)


def _s5(proj3, bbd, cbd, tabs, d_skip, w_glu, b_glu):
    bsz, seq, _ = proj3.shape
    bw = BRANCH_WIDTH
    ns = S5_GROUPS * S5_STATE
    tile = min(256, seq)
    const = lambda shape: pl.BlockSpec(shape, lambda b, i: (0,) * len(shape))
    return pl.pallas_call(
        functools.partial(_s5_kernel, tile=tile),
        grid=(bsz, seq // tile),
        in_specs=[pl.BlockSpec((1, tile, bw), lambda b, i: (b, i, GATE_COLS // bw)),
                  const((bw, 2 * ns)), const((8, S5_ROWS, ns)), const((2 * ns, bw)),
                  const((1, bw)), const((bw, bw)), const((1, bw))],
        out_specs=pl.BlockSpec((1, tile, bw), lambda b, i: (b, i, 0)),
        out_shape=jax.ShapeDtypeStruct((bsz, seq, bw), BF16),
        scratch_shapes=[pltpu.VMEM((tile, 2 * ns), F32), pltpu.VMEM((tile, 2 * ns), F32),
                        pltpu.VMEM((S5_ROWS, 2 * ns), F32)],
        compiler_params=_cparams(("parallel", "arbitrary")),
        name="s5",
    )(proj3, bbd, tabs, cbd, d_skip.astype(F32).reshape(1, bw), w_glu.astype(BF16),
      b_glu.astype(F32).reshape(1, bw))


def _hgrn_kernel(q_ref, f_ref, i_ref, g_ref, lb_ref, gain_ref, tril_ref, o_ref, st_scr, *, seq):
    c, sub = HG_CHUNK, HG_SUB
    lb = lb_ref[0]
    gain = gain_ref[0]
    tril = tril_ref[...]
    row = lax.broadcasted_iota(jnp.int32, (c, LANE), 0)
    rsub = jnp.bitwise_and(row, sub - 1)
    tcol = lax.broadcasted_iota(jnp.int32, (sub, c), 1)
    st_scr[...] = jnp.zeros_like(st_scr)
    nt = (((1,), (1,)), ((), ()))

    def chunk(ci, carry):
        off = pl.multiple_of(ci * c, c)
        q = _silu(q_ref[0, pl.ds(off, c), :].astype(F32))
        fg = lb + (1.0 - lb) * _sigmoid(f_ref[0, pl.ds(off, c), :].astype(F32))
        logf = jnp.log(jnp.maximum(fg, HG_F_MIN))
        k = 1.0 - fg
        v = i_ref[0, pl.ds(off, c), :]
        vf = v.astype(F32)
        bc = _dot01_left(tril, logf)
        o = jnp.zeros((c, LANE), F32)
        for lag in range(sub):
            if lag == 0:
                kd, bd, vd = k, bc, vf
            else:
                kd = pltpu.roll(k, lag, 0)
                bd = pltpu.roll(bc, lag, 0)
                vd = pltpu.roll(vf, lag, 0)
            valid = rsub >= lag
            e = jnp.exp(jnp.where(valid, bc - bd, 0.0))
            w = jnp.sum(q * kd * e, axis=-1, keepdims=True)
            o = o + jnp.where(valid, w * vd, 0.0)
        parts = []
        for j in range(c // sub - 1):
            lo, hi = sub * j, sub * (j + 1)
            bend = bc[hi - 1:hi, :]
            qj = (q * jnp.exp(jnp.minimum(bc - bend, 0.0))).astype(BF16)
            kj = (k[lo:hi] * jnp.exp(bend - bc[lo:hi])).astype(BF16)
            st = lax.dot_general(kj, qj, nt, preferred_element_type=F32)
            parts.append(jnp.where(tcol >= hi, st, 0.0))
        parts.append(jnp.zeros((sub, c), F32))
        scores = jnp.concatenate(parts, axis=0).T
        o = o + jnp.dot(scores.astype(BF16), v, preferred_element_type=F32)
        st_prev = st_scr[...]
        qe = (q * jnp.exp(bc)).astype(BF16)
        o = o + lax.dot_general(qe, st_prev.astype(BF16), nt, preferred_element_type=F32)
        blast = bc[c - 1:c, :]
        kdec = (k * jnp.exp(blast - bc)).astype(BF16)
        st_scr[...] = st_prev * jnp.exp(blast) + jnp.dot(vf.T.astype(BF16), kdec,
                                                         preferred_element_type=F32)
        o = o * lax.rsqrt(jnp.mean(o * o, axis=-1, keepdims=True) + EPS) * gain
        o = o * _silu(g_ref[0, pl.ds(off, c), :].astype(F32))
        o_ref[0, pl.ds(off, c), :] = o.astype(o_ref.dtype)
        return carry

    lax.fori_loop(0, seq // c, chunk, 0)


def _hgrn(proj3, lower_bound, norm_gain):
    bsz, seq, _ = proj3.shape
    c = HG_CHUNK
    tril = jnp.tril(jnp.ones((c, c), F32)).astype(BF16)
    base = PROJ_BLK0 + BRANCH_WIDTH // LANE

    def col(k):
        return pl.BlockSpec((1, seq, LANE), lambda b, h: (b, 0, base + k * HG_HEADS + h))

    vec = pl.BlockSpec((1, 1, LANE), lambda b, h: (h, 0, 0))
    return pl.pallas_call(
        functools.partial(_hgrn_kernel, seq=seq),
        grid=(bsz, HG_HEADS),
        in_specs=[col(0), col(1), col(2), col(3), vec, vec,
                  pl.BlockSpec((c, c), lambda b, h: (0, 0))],
        out_specs=pl.BlockSpec((1, seq, LANE), lambda b, h: (b, 0, h)),
        out_shape=jax.ShapeDtypeStruct((bsz, seq, BRANCH_WIDTH), BF16),
        scratch_shapes=[pltpu.VMEM((LANE, LANE), F32)],
        compiler_params=_cparams(("parallel", "parallel")),
        name="hgrn2",
    )(proj3, proj3, proj3, proj3, lower_bound.reshape(HG_HEADS, 1, LANE),
      norm_gain.reshape(HG_HEADS, 1, LANE), tril)


def _sb_kernel(q_ref, k_ref, v_ref, up_ref, o_ref, acc_scr, later_scr):
    qi = pl.program_id(1)
    t = SB_TILE
    up = up_ref[...]
    scale = 1.0 / math.sqrt(SB_DIM)
    row = lax.broadcasted_iota(jnp.int32, (t, t), 0)
    colv = lax.broadcasted_iota(jnp.int32, (t, t), 1)
    below = colv < row
    nt = (((1,), (1,)), ((), ()))
    acc_scr[...] = jnp.zeros_like(acc_scr)
    later_scr[...] = jnp.zeros_like(later_scr)

    def tile(jj, diagonal):
        off = pl.multiple_of((qi - jj) * t, t)
        keep = (lambda a: jnp.where(below, a, 0.0)) if diagonal else (lambda a: a)
        heads = [slice(h * SB_DIM, (h + 1) * SB_DIM) for h in range(SB_HEADS)]
        zs = [lax.dot_general(q_ref[0, :, hl], k_ref[0, pl.ds(off, t), hl], nt,
                              preferred_element_type=F32) * scale for hl in heads]
        logits, suffixes, totals = [], [], []
        for h in range(SB_HEADS):
            z = zs[h]
            sp = jnp.maximum(z, 0.0) + jnp.log(1.0 + jnp.exp(-jnp.abs(z)))
            log_not = keep(-sp)
            suffixes.append(_dot01_right2(log_not, up))
            logits.append(z - sp)
            totals.append(jnp.sum(log_not, axis=-1, keepdims=True))
        for h in range(SB_HEADS):
            later = later_scr[h]
            w = keep(jnp.exp(logits[h] + suffixes[h] + later))
            acc_scr[h] += jnp.dot(w.astype(BF16), v_ref[0, pl.ds(off, t), heads[h]],
                                  preferred_element_type=F32)
            later_scr[h] = later + totals[h]

    def body(jj, carry):
        tile(jj, diagonal=False)
        return carry

    tile(0, diagonal=True)
    lax.fori_loop(1, qi + 1, body, 0)
    for h in range(SB_HEADS):
        o_ref[0, :, h * SB_DIM:(h + 1) * SB_DIM] = acc_scr[h].astype(o_ref.dtype)


def _stick_breaking(proj3):
    bsz, seq, _ = proj3.shape
    t = SB_TILE
    bw = BRANCH_WIDTH
    base = (GATE_COLS + 5 * bw) // bw
    up = (jnp.arange(t)[:, None] > jnp.arange(t)[None, :]).astype(BF16)
    return pl.pallas_call(
        _sb_kernel,
        grid=(bsz, seq // t),
        in_specs=[pl.BlockSpec((1, t, bw), lambda b, i: (b, i, base)),
                  pl.BlockSpec((1, seq, bw), lambda b, i: (b, 0, base + 1)),
                  pl.BlockSpec((1, seq, bw), lambda b, i: (b, 0, base + 2)),
                  pl.BlockSpec((t, t), lambda b, i: (0, 0))],
        out_specs=pl.BlockSpec((1, t, bw), lambda b, i: (b, i, 0)),
        out_shape=jax.ShapeDtypeStruct((bsz, seq, bw), BF16),
        scratch_shapes=[pltpu.VMEM((SB_HEADS, t, SB_DIM), F32), pltpu.VMEM((SB_HEADS, t, 1), F32)],
        compiler_params=_cparams(("parallel", "arbitrary")),
        name="stick_breaking",
    )(proj3, proj3, proj3, up)


def _pool_kernel(p_ref, bc_ref, bp_ref, w_ref, sc_ref, o_ref, *, seq):
    g = pl.program_id(1)
    tile = LANE
    window = lax.shift_left(jnp.int32(2), g)
    row = lax.broadcasted_iota(jnp.int32, (tile, LANE), 0)
    band_cur = bc_ref[0]
    band_prev = bp_ref[0]
    wmix = w_ref[0]
    scale = sc_ref[0]

    group = min(4, seq // tile)

    def body(it, carry):
        offs, curs, wins = [], [], []
        for k in range(group):
            i = it * group + k
            off = pl.multiple_of(i * tile, tile)
            poff = pl.multiple_of(jnp.maximum(i - 1, 0) * tile, tile)
            cur = p_ref[0, pl.ds(off, tile), :]
            prev = p_ref[0, pl.ds(poff, tile), :]
            win = jnp.dot(band_cur, cur, preferred_element_type=F32)
            win = win + jnp.where(i > 0, jnp.dot(band_prev, prev, preferred_element_type=F32), 0.0)
            offs.append(off)
            curs.append(cur)
            wins.append(win)
        for k in range(group):
            count = jnp.minimum(row + (offs[k] + 1), window).astype(F32)
            pooled = wins[k] / count - curs[k].astype(F32)
            mixed = jnp.dot(pooled.astype(BF16), wmix, preferred_element_type=F32) * scale
            o_ref[0, pl.ds(offs[k], tile), :] = mixed.astype(o_ref.dtype)
        return carry

    lax.fori_loop(0, seq // (tile * group), body, 0)


def _pool(proj3, pool_w, pool_scale):
    bsz, seq, _ = proj3.shape
    ng = len(POOL_WINDOWS)
    base = PROJ_BLK0 + 8 * BRANCH_WIDTH // LANE
    t = jnp.arange(LANE)
    lag = t[:, None] - t[None, :]
    wins = jnp.asarray(POOL_WINDOWS)[:, None, None]
    band_cur = ((lag[None] >= 0) & (lag[None] < wins)).astype(BF16)
    band_prev = (((lag[None] + LANE) >= 0) & ((lag[None] + LANE) < wins)).astype(BF16)
    mat = pl.BlockSpec((1, LANE, LANE), lambda b, g: (g, 0, 0))
    return pl.pallas_call(
        functools.partial(_pool_kernel, seq=seq),
        grid=(bsz, ng),
        in_specs=[pl.BlockSpec((1, seq, LANE), lambda b, g: (b, 0, base + g)),
                  mat, mat, mat,
                  pl.BlockSpec((1, 1, LANE), lambda b, g: (g, 0, 0))],
        out_specs=pl.BlockSpec((1, seq, LANE), lambda b, g: (b, 0, g)),
        out_shape=jax.ShapeDtypeStruct((bsz, seq, BRANCH_WIDTH), BF16),
        compiler_params=_cparams(("parallel", "parallel")),
        name="pool",
    )(proj3, band_cur, band_prev, pool_w.astype(BF16), pool_scale.astype(F32).reshape(ng, 1, LANE))


def _merge_kernel(x_ref, g1_ref, s5_ref, hg_ref, sb_ref, po_ref, gt0_ref, gt1_ref, gt2_ref, gt3_ref,
                  wbr_ref, wout_ref, o_ref):
    dot = functools.partial(jnp.dot, preferred_element_type=F32)
    branches = (s5_ref, hg_ref, sb_ref, po_ref)
    gates = (gt0_ref, gt1_ref, gt2_ref, gt3_ref)
    merged = None
    for n in range(N_BRANCH):
        term = gates[n][...].astype(F32) * dot(branches[n][...], wbr_ref[n])
        merged = term if merged is None else merged + term
    mix = dot(merged.astype(BF16), wout_ref[...])
    o_ref[...] = x_ref[...] + g1_ref[0] * mix


def _merge(x2d, g1, s5o, hg, sb, po, proj, w_branch, w_out, seq):
    t, d = x2d.shape
    bw = BRANCH_WIDTH
    tm = min(256, seq)
    per_b = seq // tm
    tok = lambda w: pl.BlockSpec((tm, w), lambda i: (i, 0))
    gate = lambda n: pl.BlockSpec((tm, d), lambda i, n=n: (i, n))
    const = lambda shape: pl.BlockSpec(shape, lambda i: (0,) * len(shape))
    return pl.pallas_call(
        _merge_kernel,
        grid=(t // tm,),
        in_specs=[tok(d), pl.BlockSpec((1, 1, d), lambda i: (i // per_b, 0, 0)),
                  tok(bw), tok(bw), tok(bw), tok(bw),
                  gate(0), gate(1), gate(2), gate(3),
                  const((N_BRANCH, bw, d)), const((d, d))],
        out_specs=tok(d),
        out_shape=jax.ShapeDtypeStruct((t, d), F32),
        compiler_params=_cparams(("parallel",)),
        name="merge",
    )(x2d, g1, s5o, hg, sb, po, proj, proj, proj, proj, w_branch.astype(BF16), w_out.astype(BF16))


def _top_values(s, n):
    vals = []
    for r in range(n):
        cur = s if r == 0 else jnp.where(s < vals[-1], s, NEG_BIG)
        vals.append(jnp.max(cur, axis=0, keepdims=True))
    return vals


def _prefix_length(rows, pred):
    c1 = pred(rows[7])
    c2 = pred(jnp.where(c1, rows[11], rows[3]))
    c3 = pred(jnp.where(c1, jnp.where(c2, rows[13], rows[9]), jnp.where(c2, rows[5], rows[1])))
    pick = lambda a: jnp.where(c3, rows[a + 2], rows[a])
    c4 = pred(jnp.where(c1, jnp.where(c2, pick(12), pick(8)), jnp.where(c2, pick(4), pick(0))))
    c5 = pred(rows[15])
    one = lambda c, v: jnp.where(c, v, 0.0)
    return (one(c1, 8.0) + one(c2, 4.0)) + (one(c3, 2.0) + one(c4, 1.0)) + one(c5, 1.0)


def _route_kernel(x_ref, gain_ref, sh_ref, sc_ref, wq_ref, keys_ref,
                  ht_ref, cnt_ref, e1_ref, rank_ref, e2_ref):
    k = PEER_TOPK
    h = _rms_modulate(x_ref[...], gain_ref[...], sh_ref[0], sc_ref[0])
    ht = h.T.astype(BF16)
    ht_ref[...] = ht
    qt = jnp.dot(wq_ref[...], ht, preferred_element_type=F32)
    for hd in range(PEER_HEADS):
        sc = []
        for half in range(2):
            lo = (2 * hd + half) * PEER_NKEYS
            qs = qt[lo:lo + PEER_NKEYS, :].astype(BF16)
            sc.append(jnp.dot(keys_ref[hd, half], qs, preferred_element_type=F32))
        top1 = _top_values(sc[0], k)
        top2 = _top_values(sc[1], k)
        v1 = jnp.concatenate(top1, axis=0)
        v2 = jnp.concatenate(top2, axis=0)
        cand = [top1[0] + v2] + [top1[a] + v2[:8] for a in range(1, 8)] + [v1[8:] + top2[0]]
        best = _top_values(jnp.concatenate(cand, axis=0), k)
        tau = best[k - 1]
        z = None
        for r in range(k):
            e = jnp.exp(best[r] - best[0])
            z = e if z is None else z + e
        s1, s2 = sc
        cnt_ref[hd] = _prefix_length(top2, lambda row: s1 + row >= tau)
        rank_ref[hd] = _prefix_length(top2, lambda row: row > s2).astype(BF16)
        e1_ref[hd] = jnp.exp(s1 - top1[0]) / z
        e2_ref[hd] = jnp.exp(s2 - top2[0]).astype(BF16)


def _route(x2d, gain, shift, scale, wq_t, keys, seq):
    t, d = x2d.shape
    tm = min(256, seq)
    per_b = seq // tm
    nh, nk = PEER_HEADS, PEER_NKEYS
    big = pl.BlockSpec((nh, nk, tm), lambda i: (0, 0, i))
    big_f32 = jax.ShapeDtypeStruct((nh, nk, t), F32)
    big_bf16 = jax.ShapeDtypeStruct((nh, nk, t), BF16)
    return pl.pallas_call(
        _route_kernel,
        grid=(t // tm,),
        in_specs=[pl.BlockSpec((tm, d), lambda i: (i, 0)),
                  pl.BlockSpec((1, d), lambda i: (0, 0)),
                  pl.BlockSpec((1, 1, d), lambda i: (i // per_b, 0, 0)),
                  pl.BlockSpec((1, 1, d), lambda i: (i // per_b, 0, 0)),
                  pl.BlockSpec(wq_t.shape, lambda i: (0, 0)),
                  pl.BlockSpec(keys.shape, lambda i: (0, 0, 0, 0))],
        out_specs=[pl.BlockSpec((d, tm), lambda i: (0, i)), big, big, big, big],
        out_shape=[jax.ShapeDtypeStruct((d, t), BF16), big_f32, big_f32, big_bf16, big_bf16],
        compiler_params=_cparams(("parallel",)),
        name="peer_route",
    )(x2d, gain.reshape(1, d), shift, scale, wq_t, keys)


def _rows_bf16(row, n):
    tile = jnp.broadcast_to(row, (16, row.shape[1])).astype(BF16)
    return jnp.concatenate([tile] * (n // 16), axis=0)


def _peer_kernel(ht_ref, u_ref, vt_ref, cnt_ref, e1_ref, rank_ref, e2_ref, x_ref, g2_ref, fg_ref,
                 o_ref, acc_ref, *, te, tm, final_norm):
    e = pl.program_id(1)
    nk = PEER_NKEYS
    blk = PEER_BLOCK
    rows = blk // nk
    dot = functools.partial(jnp.dot, preferred_element_type=F32)

    @pl.when(e == 0)
    def _():
        acc_ref[...] = jnp.zeros_like(acc_ref)

    nblk = te // blk

    def first(c):
        return dot(u_ref[c * blk:(c + 1) * blk, :], ht_ref[...])

    def gated(c, act):
        parts = []
        for ii in range(rows):
            i = (e * nblk + c) * rows + ii
            gate = jnp.zeros((nk, tm), BF16)
            for hd in range(PEER_HEADS):
                count = _rows_bf16(cnt_ref[hd, pl.ds(i, 1), :], nk)
                e1 = _rows_bf16(e1_ref[hd, pl.ds(i, 1), :], nk)
                gate = gate + jnp.where(rank_ref[hd] < count, e1 * e2_ref[hd], jnp.zeros((), BF16))
            parts.append(gate * _gelu(act[ii * nk:(ii + 1) * nk, :]).astype(BF16))
        return jnp.concatenate(parts, axis=0)

    ahead = 2
    acts = {c: first(c) for c in range(min(ahead, nblk))}
    for c in range(nblk):
        p = gated(c, acts.pop(c))
        if c + ahead < nblk:
            acts[c + ahead] = first(c + ahead)
        acc_ref[...] += dot(vt_ref[:, c * blk:(c + 1) * blk], p)

    @pl.when(e == pl.num_programs(1) - 1)
    def _():
        y = x_ref[...] + g2_ref[0] * acc_ref[...].T
        if final_norm:
            y = y * lax.rsqrt(jnp.mean(y * y, axis=-1, keepdims=True) + EPS) * fg_ref[...]
        o_ref[...] = y


def _peer(ht, u_bf, v_bf, cnt, e1, rank2, e2, x2d, g2, final_gain, final_norm, seq):
    t, d = x2d.shape
    ne = u_bf.shape[0]
    tm = min(512, seq)
    te = 1024
    per_b = seq // tm
    nh, nk = PEER_HEADS, PEER_NKEYS
    big = pl.BlockSpec((nh, nk, tm), lambda i, g: (0, 0, i))
    return pl.pallas_call(
        functools.partial(_peer_kernel, te=te, tm=tm, final_norm=final_norm),
        grid=(t // tm, ne // te),
        in_specs=[pl.BlockSpec((d, tm), lambda i, g: (0, i)),
                  pl.BlockSpec((te, d), lambda i, g: (g, 0)),
                  pl.BlockSpec((d, te), lambda i, g: (0, g)),
                  big, big, big, big,
                  pl.BlockSpec((tm, d), lambda i, g: (i, 0), pipeline_mode=pl.Buffered(1)),
                  pl.BlockSpec((1, 1, d), lambda i, g: (i // per_b, 0, 0)),
                  pl.BlockSpec((1, d), lambda i, g: (0, 0))],
        out_specs=pl.BlockSpec((tm, d), lambda i, g: (i, 0)),
        out_shape=jax.ShapeDtypeStruct((t, d), F32),
        scratch_shapes=[pltpu.VMEM((d, tm), F32)],
        compiler_params=_cparams(("parallel", "arbitrary"), vmem=PEER_VMEM_LIMIT),
        name="peer_experts",
    )(ht, u_bf, v_bf.T, cnt, e1, rank2, e2, x2d, g2, final_gain.astype(F32).reshape(1, d))


def _layer(x2d, c, bsz, seq, ada_w, ada_b, norm_mix_gain, norm_ffn_gain, w_in, s5_params, s5_d,
           s5_w_glu, s5_b_glu, lower_bound, hg_norm_gain, pool_w, pool_scale, w_gate, w_branch, w_out,
           peer_w_query, peer_sub_keys, peer_u, peer_v, final_gain, last):
    d = D_MODEL
    mod = _ada(c.astype(F32), ada_w.astype(F32), ada_b.astype(F32))
    sh1, sc1, g1, sh2, sc2, g2 = [m.reshape(bsz, 1, d) for m in jnp.split(mod, 6, axis=-1)]
    wcat = jnp.concatenate([w_gate[n] for n in range(N_BRANCH)] + [w_in], axis=1).astype(BF16)
    proj = _inproj(x2d, norm_mix_gain.astype(F32), sh1, sc1, wcat, seq)
    proj3 = proj.reshape(bsz, seq, CAT_COLS)
    bbd, cbd, tabs = _s5_tables(*s5_params)
    s5o = _s5(proj3, bbd, cbd, tabs, s5_d, s5_w_glu, s5_b_glu)
    hg = _hgrn(proj3, lower_bound, hg_norm_gain.astype(F32))
    sb = _stick_breaking(proj3)
    po = _pool(proj3, pool_w, pool_scale)
    t = bsz * seq
    flat = lambda a: a.reshape(t, BRANCH_WIDTH)
    x1 = _merge(x2d, g1, flat(s5o), flat(hg), flat(sb), flat(po), proj, w_branch, w_out, seq)
    ht, cnt, e1, rank2, e2 = _route(x1, norm_ffn_gain.astype(F32), sh2, sc2,
                                    peer_w_query.T.astype(BF16), peer_sub_keys.astype(BF16), seq)
    return _peer(ht, peer_u.astype(BF16), peer_v.astype(BF16), cnt, e1, rank2, e2, x1, g2, final_gain, last,
                 seq)


def kernel(x, c, ada_w, ada_b, norm_mix_gain, norm_ffn_gain, w_in, s5_lambda_re, s5_lambda_im, s5_log_step, s5_b_re, s5_b_im, s5_c_re, s5_c_im, s5_d, s5_w_glu, s5_b_glu, hg_lb_logits, hg_norm_gain, pool_w, pool_scale, w_gate, w_branch, w_out, peer_w_query, peer_sub_keys, peer_u, peer_v, final_gain):
    bsz, seq, d = x.shape
    depth = ada_w.shape[0]
    lb_soft = jax.nn.softmax(hg_lb_logits.astype(F32), axis=0)
    lower_bounds = jnp.cumsum(lb_soft, axis=0) - lb_soft[0:1]
    x2d = x.astype(F32).reshape(bsz * seq, d)
    for l in range(depth):
        s5_params = (s5_lambda_re[l], s5_lambda_im[l], s5_log_step[l], s5_b_re[l], s5_b_im[l],
                     s5_c_re[l], s5_c_im[l])
        x2d = _layer(x2d, c, bsz, seq, ada_w[l], ada_b[l], norm_mix_gain[l], norm_ffn_gain[l], w_in[l],
                     s5_params, s5_d[l], s5_w_glu[l], s5_b_glu[l], lower_bounds[l], hg_norm_gain[l],
                     pool_w[l], pool_scale[l], w_gate[l], w_branch[l], w_out[l], peer_w_query[l],
                     peer_sub_keys[l], peer_u[l], peer_v[l], final_gain, l == depth - 1)
    return x2d.reshape(bsz, seq, d).astype(x.dtype)
```

```python
import functools
import math

import jax
import jax.numpy as jnp
from jax import lax
from jax.experimental import pallas as pl
from jax.experimental.pallas import tpu as pltpu

F32 = jnp.float32
BF16 = jnp.bfloat16

D_MODEL = 2048
BRANCH_WIDTH = 512
N_BRANCH = 4
LANE = 128
S5_GROUP = 16
S5_GROUPS = 32
S5_STATE = 64
S5_MIN_NEG = 1e-4
S5_ROWS = 8
HG_HEADS = 4
HG_F_MIN = 1e-6
HG_K_MIN = 1e-30
HG_CHUNK = 128
HG_SUB = 16
SB_HEADS = 4
SB_DIM = 128
SB_TILE = 256
POOL_WINDOWS = (2, 4, 8, 16)
PEER_HEADS = 8
PEER_NKEYS = 128
PEER_TOPK = 16
PEER_BLOCK = 256
EPS = 1e-6
NEG_BIG = -3.0e38

GATE_COLS = N_BRANCH * D_MODEL
IN_COLS = 9 * BRANCH_WIDTH
CAT_COLS = GATE_COLS + IN_COLS
PROJ_BLK0 = GATE_COLS // LANE
INPROJ_CHUNK = 256
VMEM_LIMIT = 48 * 1024 * 1024
PEER_VMEM_LIMIT = 56 * 1024 * 1024


def _cparams(semantics, vmem=VMEM_LIMIT):
    return pltpu.CompilerParams(dimension_semantics=semantics, vmem_limit_bytes=vmem)


def _split3(x):
    hi = x.astype(BF16)
    r1 = x - hi.astype(F32)
    mid = r1.astype(BF16)
    lo = (r1 - mid.astype(F32)).astype(BF16)
    return hi, mid, lo


def _dot01_left(m01, x):
    hi, mid, lo = _split3(x)
    d = functools.partial(jnp.dot, preferred_element_type=F32)
    return d(m01, hi) + d(m01, mid) + d(m01, lo)


def _dot01_right2(x, m01):
    hi = x.astype(BF16)
    lo = (x - hi.astype(F32)).astype(BF16)
    d = functools.partial(jnp.dot, preferred_element_type=F32)
    return d(hi, m01) + d(lo, m01)


def _dot_f32(a, b):
    ah, am, al = _split3(a)
    bh, bm, bl = _split3(b)
    d = functools.partial(jnp.dot, preferred_element_type=F32)
    return (d(ah, bh) + (d(ah, bm) + d(am, bh))) + ((d(ah, bl) + d(al, bh)) + d(am, bm))


def _gelu(x):
    return 0.5 * x * (1.0 + lax.erf(x * (1.0 / math.sqrt(2.0))))


def _sigmoid(x):
    return 0.5 * jnp.tanh(0.5 * x) + 0.5


def _silu(x):
    return x * _sigmoid(x)


def _rms_modulate(x, gain, shift, scale):
    y = x * lax.rsqrt(jnp.mean(x * x, axis=-1, keepdims=True) + EPS) * gain
    return y * (1.0 + scale) + shift


def _ada_kernel(c_ref, w_ref, b_ref, o_ref):
    o_ref[...] = _dot_f32(_silu(c_ref[...]), w_ref[...]) + b_ref[...]


def _ada(c, w, b):
    bsz, d = c.shape
    n = w.shape[1]
    tn = 1024
    return pl.pallas_call(
        _ada_kernel,
        grid=(n // tn,),
        in_specs=[pl.BlockSpec((bsz, d), lambda j: (0, 0)),
                  pl.BlockSpec((d, tn), lambda j: (0, j)),
                  pl.BlockSpec((1, tn), lambda j: (0, j))],
        out_specs=pl.BlockSpec((bsz, tn), lambda j: (0, j)),
        out_shape=jax.ShapeDtypeStruct((bsz, n), F32),
        compiler_params=_cparams(("arbitrary",)),
        name="ada_mod",
    )(c, w, b.reshape(1, n))


def _inproj_kernel(x_ref, gain_ref, sh_ref, sc_ref, w_ref, o_ref, h_ref, *, gate_chunks):
    j = pl.program_id(1)

    @pl.when(j == 0)
    def _():
        def rows(r, c):
            sl = pl.ds(pl.multiple_of(r * LANE, LANE), LANE)
            h_ref[sl, :] = _rms_modulate(x_ref[sl, :], gain_ref[...], sh_ref[0], sc_ref[0]).astype(BF16)
            return c

        lax.fori_loop(0, x_ref.shape[0] // LANE, rows, 0)

    cw = INPROJ_CHUNK
    nch = w_ref.shape[1] // cw
    mm = lambda k: jnp.dot(h_ref[...], w_ref[:, k * cw:(k + 1) * cw], preferred_element_type=F32)
    parts = {0: mm(0), 1: mm(1)}
    for k in range(nch):
        r = parts.pop(k)
        if k + 2 < nch:
            parts[k + 2] = mm(k + 2)
        is_gate = j * nch + k < gate_chunks
        o_ref[:, k * cw:(k + 1) * cw] = jnp.where(is_gate, _sigmoid(r), r).astype(o_ref.dtype)


def _inproj(x2d, gain, shift, scale, wcat, seq):
    t, d = x2d.shape
    n = wcat.shape[1]
    tm = min(1024, seq)
    tn = 5 * INPROJ_CHUNK
    per_b = seq // tm
    return pl.pallas_call(
        functools.partial(_inproj_kernel, gate_chunks=GATE_COLS // INPROJ_CHUNK),
        grid=(t // tm, n // tn),
        in_specs=[pl.BlockSpec((tm, d), lambda i, j: (i, 0)),
                  pl.BlockSpec((1, d), lambda i, j: (0, 0)),
                  pl.BlockSpec((1, 1, d), lambda i, j: (i // per_b, 0, 0)),
                  pl.BlockSpec((1, 1, d), lambda i, j: (i // per_b, 0, 0)),
                  pl.BlockSpec((d, tn), lambda i, j: (0, j))],
        out_specs=pl.BlockSpec((tm, tn), lambda i, j: (i, j)),
        out_shape=jax.ShapeDtypeStruct((t, n), BF16),
        scratch_shapes=[pltpu.VMEM((tm, d), BF16)],
        compiler_params=_cparams(("parallel", "arbitrary")),
        name="inproj",
    )(x2d, gain.reshape(1, d), shift, scale, wcat)


def _s5_tables(lam_re, lam_im, log_step, b_re, b_im, c_re, c_im):
    g, p, h = S5_GROUPS, S5_STATE, S5_GROUP
    lr = jnp.minimum(lam_re.astype(F32), -S5_MIN_NEG)
    li = lam_im.astype(F32)
    step = jnp.exp(log_step.astype(F32))[:, None]
    mag = jnp.exp(lr * step)
    ab_re = mag * jnp.cos(li * step)
    ab_im = mag * jnp.sin(li * step)
    den = lr * lr + li * li
    nr = ab_re - 1.0
    fr = (nr * lr + ab_im * li) / den
    fi = (ab_im * lr - nr * li) / den
    br, bi = b_re.astype(F32), b_im.astype(F32)
    bb_re = fr[..., None] * br - fi[..., None] * bi
    bb_im = fr[..., None] * bi + fi[..., None] * br
    eye_g = jnp.eye(g, dtype=F32)

    def in_map(m):
        return jnp.einsum('gph,gk->ghkp', m, eye_g).reshape(g * h, g * p)

    def out_map(m):
        return jnp.einsum('ghp,gk->gpkh', m, eye_g).reshape(g * p, g * h)

    bbd = jnp.concatenate([in_map(bb_re), in_map(bb_im)], axis=1).astype(BF16)
    cbd = jnp.concatenate([out_map(c_re.astype(F32)), out_map(-c_im.astype(F32))], axis=0).astype(BF16)
    prs, pis = [ab_re], [ab_im]
    for _ in range(S5_ROWS - 1):
        pr_, pi_ = prs[-1], pis[-1]
        prs.append(pr_ * ab_re - pi_ * ab_im)
        pis.append(pr_ * ab_im + pi_ * ab_re)
    rows = jnp.arange(S5_ROWS)[:, None]
    tabs = []
    for d in (1, 2, 4):
        tabs.append(jnp.where(rows >= d, prs[d - 1].reshape(1, g * p), 0.0))
        tabs.append(jnp.where(rows >= d, pis[d - 1].reshape(1, g * p), 0.0))
    tabs.append(jnp.stack(prs).reshape(S5_ROWS, g * p))
    tabs.append(jnp.stack(pis).reshape(S5_ROWS, g * p))
    return bbd, cbd, jnp.stack(tabs).astype(F32)


def _s5_kernel(u_ref, bbd_ref, tab_ref, cbd_ref, d_ref, wglu_ref, bglu_ref, o_ref,
               bu_scr, x_scr, carry_scr, *, tile):
    ns = S5_GROUPS * S5_STATE
    lc = 512

    @pl.when(pl.program_id(1) == 0)
    def _():
        carry_scr[...] = jnp.zeros_like(carry_scr)

    u = u_ref[0]
    dot = functools.partial(jnp.dot, preferred_element_type=F32)
    gl = lc // (S5_STATE // S5_GROUP)
    for ch in range(ns // lc):
        ub = u[:, ch * gl:(ch + 1) * gl]
        for part in (0, ns):
            cols = slice(part + ch * lc, part + (ch + 1) * lc)
            bu_scr[:, cols] = dot(ub, bbd_ref[ch * gl:(ch + 1) * gl, cols])

    def block(r, c):
        off = pl.multiple_of(r * S5_ROWS, S5_ROWS)
        for ch in range(ns // lc):
            re_l = slice(ch * lc, (ch + 1) * lc)
            im_l = slice(ns + ch * lc, ns + (ch + 1) * lc)
            xr = bu_scr[pl.ds(off, S5_ROWS), re_l]
            xi = bu_scr[pl.ds(off, S5_ROWS), im_l]
            for k, d in enumerate((1, 2, 4)):
                sr = pltpu.roll(xr, d, 0)
                si = pltpu.roll(xi, d, 0)
                mr = tab_ref[2 * k, :, re_l]
                mi = tab_ref[2 * k + 1, :, re_l]
                xr, xi = xr + (mr * sr - mi * si), xi + (mr * si + mi * sr)
            ar = tab_ref[6, :, re_l]
            ai = tab_ref[7, :, re_l]
            cr = carry_scr[:, re_l]
            ci = carry_scr[:, im_l]
            xr, xi = xr + (ar * cr - ai * ci), xi + (ar * ci + ai * cr)
            x_scr[pl.ds(off, S5_ROWS), re_l] = xr
            x_scr[pl.ds(off, S5_ROWS), im_l] = xi
            carry_scr[:, re_l] = jnp.broadcast_to(xr[S5_ROWS - 1:S5_ROWS], (S5_ROWS, lc))
            carry_scr[:, im_l] = jnp.broadcast_to(xi[S5_ROWS - 1:S5_ROWS], (S5_ROWS, lc))
        return c

    lax.fori_loop(0, tile // S5_ROWS, block, 0)
    ys = []
    for ch in range(ns // lc):
        out_l = slice(ch * gl, (ch + 1) * gl)
        ys.append(sum(dot(x_scr[:, part + ch * lc:part + (ch + 1) * lc].astype(BF16),
                          cbd_ref[part + ch * lc:part + (ch + 1) * lc, out_l]) for part in (0, ns)))
    y = _gelu(jnp.concatenate(ys, axis=1) + d_ref[...] * u.astype(F32))
    gate = _sigmoid(dot(y.astype(BF16), wglu_ref[...]) + bglu_ref[...])
    o_ref[0] = (y * gate).astype(o_ref.dtype)


def _s5(proj3, bbd, cbd, tabs, d_skip, w_glu, b_glu):
    bsz, seq, _ = proj3.shape
    bw = BRANCH_WIDTH
    ns = S5_GROUPS * S5_STATE
    tile = min(256, seq)
    const = lambda shape: pl.BlockSpec(shape, lambda b, i: (0,) * len(shape))
    return pl.pallas_call(
        functools.partial(_s5_kernel, tile=tile),
        grid=(bsz, seq // tile),
        in_specs=[pl.BlockSpec((1, tile, bw), lambda b, i: (b, i, GATE_COLS // bw)),
                  const((bw, 2 * ns)), const((8, S5_ROWS, ns)), const((2 * ns, bw)),
                  const((1, bw)), const((bw, bw)), const((1, bw))],
        out_specs=pl.BlockSpec((1, tile, bw), lambda b, i: (b, i, 0)),
        out_shape=jax.ShapeDtypeStruct((bsz, seq, bw), BF16),
        scratch_shapes=[pltpu.VMEM((tile, 2 * ns), F32), pltpu.VMEM((tile, 2 * ns), F32),
                        pltpu.VMEM((S5_ROWS, 2 * ns), F32)],
        compiler_params=_cparams(("parallel", "arbitrary")),
        name="s5",
    )(proj3, bbd, tabs, cbd, d_skip.astype(F32).reshape(1, bw), w_glu.astype(BF16),
      b_glu.astype(F32).reshape(1, bw))


def _hgrn_kernel(q_ref, f_ref, i_ref, g_ref, lb_ref, gain_ref, tril_ref, o_ref, st_scr, *, seq):
    c, sub = HG_CHUNK, HG_SUB
    lb = lb_ref[0]
    gain = gain_ref[0]
    tril = tril_ref[...]
    row = lax.broadcasted_iota(jnp.int32, (c, LANE), 0)
    rsub = jnp.bitwise_and(row, sub - 1)
    tcol = lax.broadcasted_iota(jnp.int32, (sub, c), 1)
    st_scr[...] = jnp.zeros_like(st_scr)
    nt = (((1,), (1,)), ((), ()))

    def chunk(ci, carry):
        off = pl.multiple_of(ci * c, c)
        q = _silu(q_ref[0, pl.ds(off, c), :].astype(F32))
        fg = lb + (1.0 - lb) * _sigmoid(f_ref[0, pl.ds(off, c), :].astype(F32))
        logf = jnp.log(jnp.maximum(fg, HG_F_MIN))
        k = 1.0 - fg
        v = i_ref[0, pl.ds(off, c), :]
        vf = v.astype(F32)
        bc = _dot01_left(tril, logf)
        bk = bc - jnp.log(jnp.maximum(k, HG_K_MIN))
        o = jnp.zeros((c, LANE), F32)
        for lag in range(sub):
            if lag == 0:
                bd, vd = bk, vf
            else:
                bd = pltpu.roll(bk, lag, 0)
                vd = pltpu.roll(vf, lag, 0)
            valid = rsub >= lag
            e = jnp.exp(jnp.where(valid, bc - bd, 0.0))
            w = jnp.sum(q * e, axis=-1, keepdims=True)
            o = o + jnp.where(valid, w * vd, 0.0)
        parts = []
        for j in range(c // sub - 1):
            lo, hi = sub * j, sub * (j + 1)
            bend = bc[hi - 1:hi, :]
            qj = (q * jnp.exp(jnp.minimum(bc - bend, 0.0))).astype(BF16)
            kj = (k[lo:hi] * jnp.exp(bend - bc[lo:hi])).astype(BF16)
            st = lax.dot_general(kj, qj, nt, preferred_element_type=F32)
            parts.append(jnp.where(tcol >= hi, st, 0.0))
        parts.append(jnp.zeros((sub, c), F32))
        scores = jnp.concatenate(parts, axis=0).T
        o = o + jnp.dot(scores.astype(BF16), v, preferred_element_type=F32)
        st_prev = st_scr[...]
        qe = (q * jnp.exp(bc)).astype(BF16)
        o = o + lax.dot_general(qe, st_prev.astype(BF16), nt, preferred_element_type=F32)
        blast = bc[c - 1:c, :]
        kdec = (k * jnp.exp(blast - bc)).astype(BF16)
        st_scr[...] = st_prev * jnp.exp(blast) + jnp.dot(vf.T.astype(BF16), kdec,
                                                         preferred_element_type=F32)
        o = o * lax.rsqrt(jnp.mean(o * o, axis=-1, keepdims=True) + EPS) * gain
        o = o * _silu(g_ref[0, pl.ds(off, c), :].astype(F32))
        o_ref[0, pl.ds(off, c), :] = o.astype(o_ref.dtype)
        return carry

    lax.fori_loop(0, seq // c, chunk, 0)


def _hgrn(proj3, lower_bound, norm_gain):
    bsz, seq, _ = proj3.shape
    c = HG_CHUNK
    tril = jnp.tril(jnp.ones((c, c), F32)).astype(BF16)
    base = PROJ_BLK0 + BRANCH_WIDTH // LANE

    def col(k):
        return pl.BlockSpec((1, seq, LANE), lambda b, h: (b, 0, base + k * HG_HEADS + h))

    vec = pl.BlockSpec((1, 1, LANE), lambda b, h: (h, 0, 0))
    return pl.pallas_call(
        functools.partial(_hgrn_kernel, seq=seq),
        grid=(bsz, HG_HEADS),
        in_specs=[col(0), col(1), col(2), col(3), vec, vec,
                  pl.BlockSpec((c, c), lambda b, h: (0, 0))],
        out_specs=pl.BlockSpec((1, seq, LANE), lambda b, h: (b, 0, h)),
        out_shape=jax.ShapeDtypeStruct((bsz, seq, BRANCH_WIDTH), BF16),
        scratch_shapes=[pltpu.VMEM((LANE, LANE), F32)],
        compiler_params=_cparams(("parallel", "parallel")),
        name="hgrn2",
    )(proj3, proj3, proj3, proj3, lower_bound.reshape(HG_HEADS, 1, LANE),
      norm_gain.reshape(HG_HEADS, 1, LANE), tril)


def _sb_kernel(q_ref, k_ref, v_ref, up_ref, o_ref, acc_scr, later_scr):
    qi = pl.program_id(1)
    t = SB_TILE
    up = up_ref[...]
    scale = 1.0 / math.sqrt(SB_DIM)
    row = lax.broadcasted_iota(jnp.int32, (t, t), 0)
    colv = lax.broadcasted_iota(jnp.int32, (t, t), 1)
    below = colv < row
    nt = (((1,), (1,)), ((), ()))
    acc_scr[...] = jnp.zeros_like(acc_scr)
    later_scr[...] = jnp.zeros_like(later_scr)

    def tile(jj, diagonal):
        off = pl.multiple_of((qi - jj) * t, t)
        keep = (lambda a: jnp.where(below, a, 0.0)) if diagonal else (lambda a: a)
        heads = [slice(h * SB_DIM, (h + 1) * SB_DIM) for h in range(SB_HEADS)]
        zs = [lax.dot_general(q_ref[0, :, hl], k_ref[0, pl.ds(off, t), hl], nt,
                              preferred_element_type=F32) * scale for hl in heads]
        logits, suffixes, totals = [], [], []
        for h in range(SB_HEADS):
            z = zs[h]
            sp = jnp.maximum(z, 0.0) + jnp.log(1.0 + jnp.exp(-jnp.abs(z)))
            log_not = keep(-sp)
            suffixes.append(_dot01_right2(log_not, up))
            logits.append(z - sp)
            totals.append(jnp.sum(log_not, axis=-1, keepdims=True))
        for h in range(SB_HEADS):
            later = later_scr[h]
            w = keep(jnp.exp(logits[h] + suffixes[h] + later))
            acc_scr[h] += jnp.dot(w.astype(BF16), v_ref[0, pl.ds(off, t), heads[h]],
                                  preferred_element_type=F32)
            later_scr[h] = later + totals[h]

    def body(jj, carry):
        tile(jj, diagonal=False)
        return carry

    tile(0, diagonal=True)
    lax.fori_loop(1, qi + 1, body, 0)
    for h in range(SB_HEADS):
        o_ref[0, :, h * SB_DIM:(h + 1) * SB_DIM] = acc_scr[h].astype(o_ref.dtype)


def _stick_breaking(proj3):
    bsz, seq, _ = proj3.shape
    t = SB_TILE
    bw = BRANCH_WIDTH
    base = (GATE_COLS + 5 * bw) // bw
    up = (jnp.arange(t)[:, None] > jnp.arange(t)[None, :]).astype(BF16)
    return pl.pallas_call(
        _sb_kernel,
        grid=(bsz, seq // t),
        in_specs=[pl.BlockSpec((1, t, bw), lambda b, i: (b, i, base)),
                  pl.BlockSpec((1, seq, bw), lambda b, i: (b, 0, base + 1)),
                  pl.BlockSpec((1, seq, bw), lambda b, i: (b, 0, base + 2)),
                  pl.BlockSpec((t, t), lambda b, i: (0, 0))],
        out_specs=pl.BlockSpec((1, t, bw), lambda b, i: (b, i, 0)),
        out_shape=jax.ShapeDtypeStruct((bsz, seq, bw), BF16),
        scratch_shapes=[pltpu.VMEM((SB_HEADS, t, SB_DIM), F32), pltpu.VMEM((SB_HEADS, t, 1), F32)],
        compiler_params=_cparams(("parallel", "arbitrary")),
        name="stick_breaking",
    )(proj3, proj3, proj3, up)


def _pool_kernel(p_ref, bc_ref, bp_ref, w_ref, sc_ref, o_ref, *, seq):
    g = pl.program_id(1)
    tile = LANE
    window = lax.shift_left(jnp.int32(2), g)
    row = lax.broadcasted_iota(jnp.int32, (tile, LANE), 0)
    band_cur = bc_ref[0]
    band_prev = bp_ref[0]
    wmix = w_ref[0]
    scale = sc_ref[0]

    group = min(4, seq // tile)

    def body(it, carry):
        offs, curs, wins = [], [], []
        for k in range(group):
            i = it * group + k
            off = pl.multiple_of(i * tile, tile)
            poff = pl.multiple_of(jnp.maximum(i - 1, 0) * tile, tile)
            cur = p_ref[0, pl.ds(off, tile), :]
            prev = p_ref[0, pl.ds(poff, tile), :]
            win = jnp.dot(band_cur, cur, preferred_element_type=F32)
            win = win + jnp.where(i > 0, jnp.dot(band_prev, prev, preferred_element_type=F32), 0.0)
            offs.append(off)
            curs.append(cur)
            wins.append(win)
        for k in range(group):
            count = jnp.minimum(row + (offs[k] + 1), window).astype(F32)
            pooled = wins[k] / count - curs[k].astype(F32)
            mixed = jnp.dot(pooled.astype(BF16), wmix, preferred_element_type=F32) * scale
            o_ref[0, pl.ds(offs[k], tile), :] = mixed.astype(o_ref.dtype)
        return carry

    lax.fori_loop(0, seq // (tile * group), body, 0)


def _pool(proj3, pool_w, pool_scale):
    bsz, seq, _ = proj3.shape
    ng = len(POOL_WINDOWS)
    base = PROJ_BLK0 + 8 * BRANCH_WIDTH // LANE
    t = jnp.arange(LANE)
    lag = t[:, None] - t[None, :]
    wins = jnp.asarray(POOL_WINDOWS)[:, None, None]
    band_cur = ((lag[None] >= 0) & (lag[None] < wins)).astype(BF16)
    band_prev = (((lag[None] + LANE) >= 0) & ((lag[None] + LANE) < wins)).astype(BF16)
    mat = pl.BlockSpec((1, LANE, LANE), lambda b, g: (g, 0, 0))
    return pl.pallas_call(
        functools.partial(_pool_kernel, seq=seq),
        grid=(bsz, ng),
        in_specs=[pl.BlockSpec((1, seq, LANE), lambda b, g: (b, 0, base + g)),
                  mat, mat, mat,
                  pl.BlockSpec((1, 1, LANE), lambda b, g: (g, 0, 0))],
        out_specs=pl.BlockSpec((1, seq, LANE), lambda b, g: (b, 0, g)),
        out_shape=jax.ShapeDtypeStruct((bsz, seq, BRANCH_WIDTH), BF16),
        compiler_params=_cparams(("parallel", "parallel")),
        name="pool",
    )(proj3, band_cur, band_prev, pool_w.astype(BF16), pool_scale.astype(F32).reshape(ng, 1, LANE))


def _merge_kernel(x_ref, g1_ref, s5_ref, hg_ref, sb_ref, po_ref, gt0_ref, gt1_ref, gt2_ref, gt3_ref,
                  wbr_ref, wout_ref, o_ref):
    dot = functools.partial(jnp.dot, preferred_element_type=F32)
    branches = (s5_ref, hg_ref, sb_ref, po_ref)
    gates = (gt0_ref, gt1_ref, gt2_ref, gt3_ref)
    merged = None
    for n in range(N_BRANCH):
        term = gates[n][...].astype(F32) * dot(branches[n][...], wbr_ref[n])
        merged = term if merged is None else merged + term
    mix = dot(merged.astype(BF16), wout_ref[...])
    o_ref[...] = x_ref[...] + g1_ref[0] * mix


def _merge(x2d, g1, s5o, hg, sb, po, proj, w_branch, w_out, seq):
    t, d = x2d.shape
    bw = BRANCH_WIDTH
    tm = min(256, seq)
    per_b = seq // tm
    tok = lambda w: pl.BlockSpec((tm, w), lambda i: (i, 0))
    gate = lambda n: pl.BlockSpec((tm, d), lambda i, n=n: (i, n))
    const = lambda shape: pl.BlockSpec(shape, lambda i: (0,) * len(shape))
    return pl.pallas_call(
        _merge_kernel,
        grid=(t // tm,),
        in_specs=[tok(d), pl.BlockSpec((1, 1, d), lambda i: (i // per_b, 0, 0)),
                  tok(bw), tok(bw), tok(bw), tok(bw),
                  gate(0), gate(1), gate(2), gate(3),
                  const((N_BRANCH, bw, d)), const((d, d))],
        out_specs=tok(d),
        out_shape=jax.ShapeDtypeStruct((t, d), F32),
        compiler_params=_cparams(("parallel",)),
        name="merge",
    )(x2d, g1, s5o, hg, sb, po, proj, proj, proj, proj, w_branch.astype(BF16), w_out.astype(BF16))


def _top_values(s, n):
    vals = []
    for r in range(n):
        cur = s if r == 0 else jnp.where(s < vals[-1], s, NEG_BIG)
        vals.append(jnp.max(cur, axis=0, keepdims=True))
    return vals


def _prefix_length(rows, pred):
    c1 = pred(rows[7])
    c2 = pred(jnp.where(c1, rows[11], rows[3]))
    c3 = pred(jnp.where(c1, jnp.where(c2, rows[13], rows[9]), jnp.where(c2, rows[5], rows[1])))
    pick = lambda a: jnp.where(c3, rows[a + 2], rows[a])
    c4 = pred(jnp.where(c1, jnp.where(c2, pick(12), pick(8)), jnp.where(c2, pick(4), pick(0))))
    c5 = pred(rows[15])
    one = lambda c, v: jnp.where(c, v, 0.0)
    return (one(c1, 8.0) + one(c2, 4.0)) + (one(c3, 2.0) + one(c4, 1.0)) + one(c5, 1.0)


def _route_kernel(x_ref, gain_ref, sh_ref, sc_ref, wq_ref, keys_ref,
                  ht_ref, cnt_ref, e1_ref, rank_ref, e2_ref):
    k = PEER_TOPK
    h = _rms_modulate(x_ref[...], gain_ref[...], sh_ref[0], sc_ref[0])
    ht = h.T.astype(BF16)
    ht_ref[...] = ht
    qt = jnp.dot(wq_ref[...], ht, preferred_element_type=F32)
    for hd in range(PEER_HEADS):
        sc = []
        for half in range(2):
            lo = (2 * hd + half) * PEER_NKEYS
            qs = qt[lo:lo + PEER_NKEYS, :].astype(BF16)
            sc.append(jnp.dot(keys_ref[hd, half], qs, preferred_element_type=F32))
        top1 = _top_values(sc[0], k)
        top2 = _top_values(sc[1], k)
        v1 = jnp.concatenate(top1, axis=0)
        v2 = jnp.concatenate(top2, axis=0)
        cand = [top1[0] + v2] + [top1[a] + v2[:8] for a in range(1, 8)] + [v1[8:] + top2[0]]
        best = _top_values(jnp.concatenate(cand, axis=0), k)
        tau = best[k - 1]
        z = None
        for r in range(k):
            e = jnp.exp(best[r] - best[0])
            z = e if z is None else z + e
        s1, s2 = sc
        cnt_ref[hd] = _prefix_length(top2, lambda row: s1 + row >= tau)
        rank_ref[hd] = _prefix_length(top2, lambda row: row > s2).astype(BF16)
        e1_ref[hd] = jnp.exp(s1 - top1[0]) / z
        e2_ref[hd] = jnp.exp(s2 - top2[0]).astype(BF16)


def _route(x2d, gain, shift, scale, wq_t, keys, seq):
    t, d = x2d.shape
    tm = min(256, seq)
    per_b = seq // tm
    nh, nk = PEER_HEADS, PEER_NKEYS
    big = pl.BlockSpec((nh, nk, tm), lambda i: (0, 0, i))
    big_f32 = jax.ShapeDtypeStruct((nh, nk, t), F32)
    big_bf16 = jax.ShapeDtypeStruct((nh, nk, t), BF16)
    return pl.pallas_call(
        _route_kernel,
        grid=(t // tm,),
        in_specs=[pl.BlockSpec((tm, d), lambda i: (i, 0)),
                  pl.BlockSpec((1, d), lambda i: (0, 0)),
                  pl.BlockSpec((1, 1, d), lambda i: (i // per_b, 0, 0)),
                  pl.BlockSpec((1, 1, d), lambda i: (i // per_b, 0, 0)),
                  pl.BlockSpec(wq_t.shape, lambda i: (0, 0)),
                  pl.BlockSpec(keys.shape, lambda i: (0, 0, 0, 0))],
        out_specs=[pl.BlockSpec((d, tm), lambda i: (0, i)), big, big, big, big],
        out_shape=[jax.ShapeDtypeStruct((d, t), BF16), big_f32, big_f32, big_bf16, big_bf16],
        compiler_params=_cparams(("parallel",)),
        name="peer_route",
    )(x2d, gain.reshape(1, d), shift, scale, wq_t, keys)


def _rows_bf16(row, n):
    tile = jnp.broadcast_to(row, (16, row.shape[1])).astype(BF16)
    return jnp.concatenate([tile] * (n // 16), axis=0)


def _peer_kernel(ht_ref, u_ref, vt_ref, cnt_ref, e1_ref, rank_ref, e2_ref, x_ref, g2_ref, fg_ref,
                 o_ref, acc_ref, *, te, tm, final_norm):
    e = pl.program_id(1)
    nk = PEER_NKEYS
    blk = PEER_BLOCK
    rows = blk // nk
    dot = functools.partial(jnp.dot, preferred_element_type=F32)

    @pl.when(e == 0)
    def _():
        acc_ref[...] = jnp.zeros_like(acc_ref)

    nblk = te // blk

    def first(c):
        return dot(u_ref[c * blk:(c + 1) * blk, :], ht_ref[...])

    def gated(c, act):
        parts = []
        for ii in range(rows):
            i = c * rows + ii
            gate = jnp.zeros((nk, tm), BF16)
            for hd in range(PEER_HEADS):
                count = _rows_bf16(cnt_ref[hd, i:i + 1, :], nk)
                e1 = _rows_bf16(e1_ref[hd, i:i + 1, :], nk)
                gate = gate + jnp.where(rank_ref[hd] < count, e1 * e2_ref[hd], jnp.zeros((), BF16))
            parts.append(gate * _gelu(act[ii * nk:(ii + 1) * nk, :]).astype(BF16))
        return jnp.concatenate(parts, axis=0)

    ahead = 2
    acts = {c: first(c) for c in range(min(ahead, nblk))}
    for c in range(nblk):
        p = gated(c, acts.pop(c))
        if c + ahead < nblk:
            acts[c + ahead] = first(c + ahead)
        acc_ref[...] += dot(vt_ref[:, c * blk:(c + 1) * blk], p)

    @pl.when(e == pl.num_programs(1) - 1)
    def _():
        y = x_ref[...] + g2_ref[0] * acc_ref[...].T
        if final_norm:
            y = y * lax.rsqrt(jnp.mean(y * y, axis=-1, keepdims=True) + EPS) * fg_ref[...]
        o_ref[...] = y


def _peer(ht, u_bf, v_bf, cnt, e1, rank2, e2, x2d, g2, final_gain, final_norm, seq):
    t, d = x2d.shape
    ne = u_bf.shape[0]
    tm = min(512, seq)
    te = 1024
    per_b = seq // tm
    nh, nk = PEER_HEADS, PEER_NKEYS
    big = pl.BlockSpec((nh, nk, tm), lambda i, g: (0, 0, i))
    rows = pl.BlockSpec((nh, te // nk, tm), lambda i, g: (0, g, i))
    return pl.pallas_call(
        functools.partial(_peer_kernel, te=te, tm=tm, final_norm=final_norm),
        grid=(t // tm, ne // te),
        in_specs=[pl.BlockSpec((d, tm), lambda i, g: (0, i)),
                  pl.BlockSpec((te, d), lambda i, g: (g, 0)),
                  pl.BlockSpec((d, te), lambda i, g: (0, g)),
                  rows, rows, big, big,
                  pl.BlockSpec((tm, d), lambda i, g: (i, 0)),
                  pl.BlockSpec((1, 1, d), lambda i, g: (i // per_b, 0, 0)),
                  pl.BlockSpec((1, d), lambda i, g: (0, 0))],
        out_specs=pl.BlockSpec((tm, d), lambda i, g: (i, 0)),
        out_shape=jax.ShapeDtypeStruct((t, d), F32),
        scratch_shapes=[pltpu.VMEM((d, tm), F32)],
        compiler_params=_cparams(("parallel", "arbitrary"), vmem=PEER_VMEM_LIMIT),
        name="peer_experts",
    )(ht, u_bf, v_bf.T, cnt, e1, rank2, e2, x2d, g2, final_gain.astype(F32).reshape(1, d))


def _layer(x2d, c, bsz, seq, ada_w, ada_b, norm_mix_gain, norm_ffn_gain, w_in, s5_params, s5_d,
           s5_w_glu, s5_b_glu, lower_bound, hg_norm_gain, pool_w, pool_scale, w_gate, w_branch, w_out,
           peer_w_query, peer_sub_keys, peer_u, peer_v, final_gain, last):
    d = D_MODEL
    mod = _ada(c.astype(F32), ada_w.astype(F32), ada_b.astype(F32))
    sh1, sc1, g1, sh2, sc2, g2 = [m.reshape(bsz, 1, d) for m in jnp.split(mod, 6, axis=-1)]
    wcat = jnp.concatenate([w_gate[n] for n in range(N_BRANCH)] + [w_in], axis=1).astype(BF16)
    proj = _inproj(x2d, norm_mix_gain.astype(F32), sh1, sc1, wcat, seq)
    proj3 = proj.reshape(bsz, seq, CAT_COLS)
    bbd, cbd, tabs = _s5_tables(*s5_params)
    s5o = _s5(proj3, bbd, cbd, tabs, s5_d, s5_w_glu, s5_b_glu)
    hg = _hgrn(proj3, lower_bound, hg_norm_gain.astype(F32))
    sb = _stick_breaking(proj3)
    po = _pool(proj3, pool_w, pool_scale)
    t = bsz * seq
    flat = lambda a: a.reshape(t, BRANCH_WIDTH)
    x1 = _merge(x2d, g1, flat(s5o), flat(hg), flat(sb), flat(po), proj, w_branch, w_out, seq)
    ht, cnt, e1, rank2, e2 = _route(x1, norm_ffn_gain.astype(F32), sh2, sc2,
                                    peer_w_query.T.astype(BF16), peer_sub_keys.astype(BF16), seq)
    return _peer(ht, peer_u.astype(BF16), peer_v.astype(BF16), cnt, e1, rank2, e2, x1, g2, final_gain, last,
                 seq)


def kernel(x, c, ada_w, ada_b, norm_mix_gain, norm_ffn_gain, w_in, s5_lambda_re, s5_lambda_im, s5_log_step, s5_b_re, s5_b_im, s5_c_re, s5_c_im, s5_d, s5_w_glu, s5_b_glu, hg_lb_logits, hg_norm_gain, pool_w, pool_scale, w_gate, w_branch, w_out, peer_w_query, peer_sub_keys, peer_u, peer_v, final_gain):
    bsz, seq, d = x.shape
    depth = ada_w.shape[0]
    lb_soft = jax.nn.softmax(hg_lb_logits.astype(F32), axis=0)
    lower_bounds = jnp.cumsum(lb_soft, axis=0) - lb_soft[0:1]
    x2d = x.astype(F32).reshape(bsz * seq, d)
    for l in range(depth):
        s5_params = (s5_lambda_re[l], s5_lambda_im[l], s5_log_step[l], s5_b_re[l], s5_b_im[l],
                     s5_c_re[l], s5_c_im[l])
        x2d = _layer(x2d, c, bsz, seq, ada_w[l], ada_b[l], norm_mix_gain[l], norm_ffn_gain[l], w_in[l],
                     s5_params, s5_d[l], s5_w_glu[l], s5_b_glu[l], lower_bounds[l], hg_norm_gain[l],
                     pool_w[l], pool_scale[l], w_gate[l], w_branch[l], w_out[l], peer_w_query[l],
                     peer_sub_keys[l], peer_u[l], peer_v[l], final_gain, l == depth - 1)
    return x2d.reshape(bsz, seq, d).astype(x.dtype)
```

```python
import functools
import math

import jax
import jax.numpy as jnp
from jax import lax
from jax.experimental import pallas as pl
from jax.experimental.pallas import tpu as pltpu

F32 = jnp.float32
BF16 = jnp.bfloat16

D_MODEL = 2048
BRANCH_WIDTH = 512
N_BRANCH = 4
LANE = 128
S5_GROUP = 16
S5_GROUPS = 32
S5_STATE = 64
S5_MIN_NEG = 1e-4
S5_ROWS = 8
HG_HEADS = 4
HG_F_MIN = 1e-6
HG_K_MIN = 1e-30
HG_CHUNK = 128
HG_SUB = 16
HG_PAIR = 4
SB_HEADS = 4
SB_DIM = 128
SB_TILE = 256
POOL_WINDOWS = (2, 4, 8, 16)
PEER_HEADS = 8
PEER_NKEYS = 128
PEER_TOPK = 16
PEER_BLOCK = 256
EPS = 1e-6
NEG_BIG = -3.0e38

GATE_COLS = N_BRANCH * D_MODEL
IN_COLS = 9 * BRANCH_WIDTH
CAT_COLS = GATE_COLS + IN_COLS
PROJ_BLK0 = GATE_COLS // LANE
INPROJ_CHUNK = 256
VMEM_LIMIT = 48 * 1024 * 1024
PEER_VMEM_LIMIT = 56 * 1024 * 1024


def _cparams(semantics, vmem=VMEM_LIMIT):
    return pltpu.CompilerParams(dimension_semantics=semantics, vmem_limit_bytes=vmem)


def _split3(x):
    hi = x.astype(BF16)
    r1 = x - hi.astype(F32)
    mid = r1.astype(BF16)
    lo = (r1 - mid.astype(F32)).astype(BF16)
    return hi, mid, lo


def _dot01_left(m01, x):
    hi, mid, lo = _split3(x)
    d = functools.partial(jnp.dot, preferred_element_type=F32)
    return d(m01, hi) + d(m01, mid) + d(m01, lo)


def _dot01_right2(x, m01):
    hi = x.astype(BF16)
    lo = (x - hi.astype(F32)).astype(BF16)
    d = functools.partial(jnp.dot, preferred_element_type=F32)
    return d(hi, m01) + d(lo, m01)


def _dot_f32(a, b):
    ah, am, al = _split3(a)
    bh, bm, bl = _split3(b)
    d = functools.partial(jnp.dot, preferred_element_type=F32)
    return (d(ah, bh) + (d(ah, bm) + d(am, bh))) + ((d(ah, bl) + d(al, bh)) + d(am, bm))


def _gelu(x):
    return 0.5 * x * (1.0 + lax.erf(x * (1.0 / math.sqrt(2.0))))


def _sigmoid(x):
    return 0.5 * jnp.tanh(0.5 * x) + 0.5


def _silu(x):
    return x * _sigmoid(x)


def _rms_modulate(x, gain, shift, scale):
    y = x * lax.rsqrt(jnp.mean(x * x, axis=-1, keepdims=True) + EPS) * gain
    return y * (1.0 + scale) + shift


def _ada_kernel(c_ref, w_ref, b_ref, o_ref):
    o_ref[...] = _dot_f32(_silu(c_ref[...]), w_ref[...]) + b_ref[...]


def _ada(c, w, b):
    bsz, d = c.shape
    n = w.shape[1]
    tn = 1024
    return pl.pallas_call(
        _ada_kernel,
        grid=(n // tn,),
        in_specs=[pl.BlockSpec((bsz, d), lambda j: (0, 0)),
                  pl.BlockSpec((d, tn), lambda j: (0, j)),
                  pl.BlockSpec((1, tn), lambda j: (0, j))],
        out_specs=pl.BlockSpec((bsz, tn), lambda j: (0, j)),
        out_shape=jax.ShapeDtypeStruct((bsz, n), F32),
        compiler_params=_cparams(("arbitrary",)),
        name="ada_mod",
    )(c, w, b.reshape(1, n))


def _inproj_kernel(x_ref, gain_ref, sh_ref, sc_ref, w_ref, o_ref, h_ref, *, gate_chunks):
    j = pl.program_id(1)

    @pl.when(j == 0)
    def _():
        def rows(r, c):
            sl = pl.ds(pl.multiple_of(r * LANE, LANE), LANE)
            h_ref[sl, :] = _rms_modulate(x_ref[sl, :], gain_ref[...], sh_ref[0], sc_ref[0]).astype(BF16)
            return c

        lax.fori_loop(0, x_ref.shape[0] // LANE, rows, 0)

    cw = INPROJ_CHUNK
    nch = w_ref.shape[1] // cw
    mm = lambda k: jnp.dot(h_ref[...], w_ref[:, k * cw:(k + 1) * cw], preferred_element_type=F32)
    parts = {0: mm(0), 1: mm(1)}
    for k in range(nch):
        r = parts.pop(k)
        if k + 2 < nch:
            parts[k + 2] = mm(k + 2)
        is_gate = j * nch + k < gate_chunks
        o_ref[:, k * cw:(k + 1) * cw] = jnp.where(is_gate, _sigmoid(r), r).astype(o_ref.dtype)


def _inproj(x2d, gain, shift, scale, wcat, seq):
    t, d = x2d.shape
    n = wcat.shape[1]
    tm = min(1024, seq)
    tn = 5 * INPROJ_CHUNK
    per_b = seq // tm
    return pl.pallas_call(
        functools.partial(_inproj_kernel, gate_chunks=GATE_COLS // INPROJ_CHUNK),
        grid=(t // tm, n // tn),
        in_specs=[pl.BlockSpec((tm, d), lambda i, j: (i, 0)),
                  pl.BlockSpec((1, d), lambda i, j: (0, 0)),
                  pl.BlockSpec((1, 1, d), lambda i, j: (i // per_b, 0, 0)),
                  pl.BlockSpec((1, 1, d), lambda i, j: (i // per_b, 0, 0)),
                  pl.BlockSpec((d, tn), lambda i, j: (0, j))],
        out_specs=pl.BlockSpec((tm, tn), lambda i, j: (i, j)),
        out_shape=jax.ShapeDtypeStruct((t, n), BF16),
        scratch_shapes=[pltpu.VMEM((tm, d), BF16)],
        compiler_params=_cparams(("parallel", "arbitrary")),
        name="inproj",
    )(x2d, gain.reshape(1, d), shift, scale, wcat)


def _s5_tables(lam_re, lam_im, log_step, b_re, b_im, c_re, c_im):
    g, p, h = S5_GROUPS, S5_STATE, S5_GROUP
    lr = jnp.minimum(lam_re.astype(F32), -S5_MIN_NEG)
    li = lam_im.astype(F32)
    step = jnp.exp(log_step.astype(F32))[:, None]
    mag = jnp.exp(lr * step)
    ab_re = mag * jnp.cos(li * step)
    ab_im = mag * jnp.sin(li * step)
    den = lr * lr + li * li
    nr = ab_re - 1.0
    fr = (nr * lr + ab_im * li) / den
    fi = (ab_im * lr - nr * li) / den
    br, bi = b_re.astype(F32), b_im.astype(F32)
    bb_re = fr[..., None] * br - fi[..., None] * bi
    bb_im = fr[..., None] * bi + fi[..., None] * br
    eye_g = jnp.eye(g, dtype=F32)

    def in_map(m):
        return jnp.einsum('gph,gk->ghkp', m, eye_g).reshape(g * h, g * p)

    def out_map(m):
        return jnp.einsum('ghp,gk->gpkh', m, eye_g).reshape(g * p, g * h)

    bbd = jnp.concatenate([in_map(bb_re), in_map(bb_im)], axis=1).astype(BF16)
    cbd = jnp.concatenate([out_map(c_re.astype(F32)), out_map(-c_im.astype(F32))], axis=0).astype(BF16)
    prs, pis = [ab_re], [ab_im]
    for _ in range(S5_ROWS - 1):
        pr_, pi_ = prs[-1], pis[-1]
        prs.append(pr_ * ab_re - pi_ * ab_im)
        pis.append(pr_ * ab_im + pi_ * ab_re)
    rows = jnp.arange(S5_ROWS)[:, None]
    tabs = []
    for d in (1, 2, 4):
        tabs.append(jnp.where(rows >= d, prs[d - 1].reshape(1, g * p), 0.0))
        tabs.append(jnp.where(rows >= d, pis[d - 1].reshape(1, g * p), 0.0))
    tabs.append(jnp.stack(prs).reshape(S5_ROWS, g * p))
    tabs.append(jnp.stack(pis).reshape(S5_ROWS, g * p))
    return bbd, cbd, jnp.stack(tabs).astype(F32)


def _s5_kernel(u_ref, bbd_ref, tab_ref, cbd_ref, d_ref, wglu_ref, bglu_ref, o_ref,
               bu_scr, x_scr, carry_scr, *, tile):
    ns = S5_GROUPS * S5_STATE
    lc = 512

    @pl.when(pl.program_id(1) == 0)
    def _():
        carry_scr[...] = jnp.zeros_like(carry_scr)

    u = u_ref[0]
    dot = functools.partial(jnp.dot, preferred_element_type=F32)
    gl = lc // (S5_STATE // S5_GROUP)
    for ch in range(ns // lc):
        ub = u[:, ch * gl:(ch + 1) * gl]
        for part in (0, ns):
            cols = slice(part + ch * lc, part + (ch + 1) * lc)
            bu_scr[:, cols] = dot(ub, bbd_ref[ch * gl:(ch + 1) * gl, cols])

    def block(r, c):
        off = pl.multiple_of(r * S5_ROWS, S5_ROWS)
        for ch in range(ns // lc):
            re_l = slice(ch * lc, (ch + 1) * lc)
            im_l = slice(ns + ch * lc, ns + (ch + 1) * lc)
            xr = bu_scr[pl.ds(off, S5_ROWS), re_l]
            xi = bu_scr[pl.ds(off, S5_ROWS), im_l]
            for k, d in enumerate((1, 2, 4)):
                sr = pltpu.roll(xr, d, 0)
                si = pltpu.roll(xi, d, 0)
                mr = tab_ref[2 * k, :, re_l]
                mi = tab_ref[2 * k + 1, :, re_l]
                xr, xi = xr + (mr * sr - mi * si), xi + (mr * si + mi * sr)
            ar = tab_ref[6, :, re_l]
            ai = tab_ref[7, :, re_l]
            cr = carry_scr[:, re_l]
            ci = carry_scr[:, im_l]
            xr, xi = xr + (ar * cr - ai * ci), xi + (ar * ci + ai * cr)
            x_scr[pl.ds(off, S5_ROWS), re_l] = xr
            x_scr[pl.ds(off, S5_ROWS), im_l] = xi
            carry_scr[:, re_l] = jnp.broadcast_to(xr[S5_ROWS - 1:S5_ROWS], (S5_ROWS, lc))
            carry_scr[:, im_l] = jnp.broadcast_to(xi[S5_ROWS - 1:S5_ROWS], (S5_ROWS, lc))
        return c

    lax.fori_loop(0, tile // S5_ROWS, block, 0)
    ys = []
    for ch in range(ns // lc):
        out_l = slice(ch * gl, (ch + 1) * gl)
        ys.append(sum(dot(x_scr[:, part + ch * lc:part + (ch + 1) * lc].astype(BF16),
                          cbd_ref[part + ch * lc:part + (ch + 1) * lc, out_l]) for part in (0, ns)))
    y = _gelu(jnp.concatenate(ys, axis=1) + d_ref[...] * u.astype(F32))
    gate = _sigmoid(dot(y.astype(BF16), wglu_ref[...]) + bglu_ref[...])
    o_ref[0] = (y * gate).astype(o_ref.dtype)


def _s5(proj3, bbd, cbd, tabs, d_skip, w_glu, b_glu):
    bsz, seq, _ = proj3.shape
    bw = BRANCH_WIDTH
    ns = S5_GROUPS * S5_STATE
    tile = min(256, seq)
    const = lambda shape: pl.BlockSpec(shape, lambda b, i: (0,) * len(shape))
    return pl.pallas_call(
        functools.partial(_s5_kernel, tile=tile),
        grid=(bsz, seq // tile),
        in_specs=[pl.BlockSpec((1, tile, bw), lambda b, i: (b, i, GATE_COLS // bw)),
                  const((bw, 2 * ns)), const((8, S5_ROWS, ns)), const((2 * ns, bw)),
                  const((1, bw)), const((bw, bw)), const((1, bw))],
        out_specs=pl.BlockSpec((1, tile, bw), lambda b, i: (b, i, 0)),
        out_shape=jax.ShapeDtypeStruct((bsz, seq, bw), BF16),
        scratch_shapes=[pltpu.VMEM((tile, 2 * ns), F32), pltpu.VMEM((tile, 2 * ns), F32),
                        pltpu.VMEM((S5_ROWS, 2 * ns), F32)],
        compiler_params=_cparams(("parallel", "arbitrary")),
        name="s5",
    )(proj3, bbd, tabs, cbd, d_skip.astype(F32).reshape(1, bw), w_glu.astype(BF16),
      b_glu.astype(F32).reshape(1, bw))


def _hgrn_kernel(q_ref, f_ref, i_ref, g_ref, lb_ref, gain_ref, tril_ref, o_ref, st_scr, *, seq):
    c, sub = HG_CHUNK, HG_SUB
    tril = tril_ref[...]
    row = lax.broadcasted_iota(jnp.int32, (c, LANE), 0)
    rsub = jnp.bitwise_and(row, sub - 1)
    tcol = lax.broadcasted_iota(jnp.int32, (sub, c), 1)
    st_scr[...] = jnp.zeros_like(st_scr)
    nt = (((1,), (1,)), ((), ()))

    def head_chunk(off, hh):
        hl = slice(hh * LANE, (hh + 1) * LANE)
        lb = lb_ref[0, :, hl]
        gain = gain_ref[0, :, hl]
        q = _silu(q_ref[0, pl.ds(off, c), hl].astype(F32))
        fg = lb + (1.0 - lb) * _sigmoid(f_ref[0, pl.ds(off, c), hl].astype(F32))
        logf = jnp.log(jnp.maximum(fg, HG_F_MIN))
        k = 1.0 - fg
        v = i_ref[0, pl.ds(off, c), hl]
        vf = v.astype(F32)
        bc = _dot01_left(tril, logf)
        bk = bc - jnp.log(jnp.maximum(k, HG_K_MIN))
        o = jnp.zeros((c, LANE), F32)
        for lag in range(sub):
            if lag == 0:
                bd, vd = bk, vf
            else:
                bd = pltpu.roll(bk, lag, 0)
                vd = pltpu.roll(vf, lag, 0)
            valid = rsub >= lag
            e = jnp.exp(jnp.where(valid, bc - bd, 0.0))
            w = jnp.sum(q * e, axis=-1, keepdims=True)
            o = o + jnp.where(valid, w * vd, 0.0)
        parts = []
        for j in range(c // sub - 1):
            lo, hi = sub * j, sub * (j + 1)
            bend = bc[hi - 1:hi, :]
            qj = (q * jnp.exp(jnp.minimum(bc - bend, 0.0))).astype(BF16)
            kj = (k[lo:hi] * jnp.exp(bend - bc[lo:hi])).astype(BF16)
            st = lax.dot_general(kj, qj, nt, preferred_element_type=F32)
            parts.append(jnp.where(tcol >= hi, st, 0.0))
        parts.append(jnp.zeros((sub, c), F32))
        scores = jnp.concatenate(parts, axis=0).T
        o = o + jnp.dot(scores.astype(BF16), v, preferred_element_type=F32)
        st_prev = st_scr[hh]
        qe = (q * jnp.exp(bc)).astype(BF16)
        o = o + lax.dot_general(qe, st_prev.astype(BF16), nt, preferred_element_type=F32)
        blast = bc[c - 1:c, :]
        kdec = (k * jnp.exp(blast - bc)).astype(BF16)
        st_scr[hh] = st_prev * jnp.exp(blast) + jnp.dot(vf.T.astype(BF16), kdec,
                                                        preferred_element_type=F32)
        o = o * lax.rsqrt(jnp.mean(o * o, axis=-1, keepdims=True) + EPS) * gain
        o = o * _silu(g_ref[0, pl.ds(off, c), hl].astype(F32))
        o_ref[0, pl.ds(off, c), hl] = o.astype(o_ref.dtype)

    def chunk(ci, carry):
        off = pl.multiple_of(ci * c, c)
        for hh in range(HG_PAIR):
            head_chunk(off, hh)
        return carry

    lax.fori_loop(0, seq // c, chunk, 0)


def _hgrn(proj3, lower_bound, norm_gain):
    bsz, seq, _ = proj3.shape
    c = HG_CHUNK
    tril = jnp.tril(jnp.ones((c, c), F32)).astype(BF16)
    base = PROJ_BLK0 + BRANCH_WIDTH // LANE

    pw = HG_PAIR * LANE

    def col(k):
        return pl.BlockSpec((1, seq, pw), lambda b, h: (b, 0, (base + k * HG_HEADS) // HG_PAIR + h))

    vec = pl.BlockSpec((1, 1, pw), lambda b, h: (h, 0, 0))
    return pl.pallas_call(
        functools.partial(_hgrn_kernel, seq=seq),
        grid=(bsz, HG_HEADS // HG_PAIR),
        in_specs=[col(0), col(1), col(2), col(3), vec, vec,
                  pl.BlockSpec((c, c), lambda b, h: (0, 0))],
        out_specs=pl.BlockSpec((1, seq, pw), lambda b, h: (b, 0, h)),
        out_shape=jax.ShapeDtypeStruct((bsz, seq, BRANCH_WIDTH), BF16),
        scratch_shapes=[pltpu.VMEM((HG_PAIR, LANE, LANE), F32)],
        compiler_params=_cparams(("parallel", "parallel")),
        name="hgrn2",
    )(proj3, proj3, proj3, proj3, lower_bound.reshape(HG_HEADS // HG_PAIR, 1, pw),
      norm_gain.reshape(HG_HEADS // HG_PAIR, 1, pw), tril)


def _sb_kernel(q_ref, k_ref, v_ref, up_ref, o_ref, acc_scr, later_scr):
    qi = pl.program_id(1)
    t = SB_TILE
    up = up_ref[...]
    scale = 1.0 / math.sqrt(SB_DIM)
    row = lax.broadcasted_iota(jnp.int32, (t, t), 0)
    colv = lax.broadcasted_iota(jnp.int32, (t, t), 1)
    below = colv < row
    nt = (((1,), (1,)), ((), ()))
    acc_scr[...] = jnp.zeros_like(acc_scr)
    later_scr[...] = jnp.zeros_like(later_scr)

    def tile(jj, diagonal):
        off = pl.multiple_of((qi - jj) * t, t)
        keep = (lambda a: jnp.where(below, a, 0.0)) if diagonal else (lambda a: a)
        heads = [slice(h * SB_DIM, (h + 1) * SB_DIM) for h in range(SB_HEADS)]
        zs = [lax.dot_general(q_ref[0, :, hl], k_ref[0, pl.ds(off, t), hl], nt,
                              preferred_element_type=F32) * scale for hl in heads]
        logits, suffixes, totals = [], [], []
        for h in range(SB_HEADS):
            z = zs[h]
            sp = jnp.maximum(z, 0.0) + jnp.log(1.0 + jnp.exp(-jnp.abs(z)))
            log_not = keep(-sp)
            suffixes.append(_dot01_right2(log_not, up))
            logits.append(z - sp)
            totals.append(jnp.sum(log_not, axis=-1, keepdims=True))
        for h in range(SB_HEADS):
            later = later_scr[h]
            w = keep(jnp.exp(logits[h] + suffixes[h] + later))
            acc_scr[h] += jnp.dot(w.astype(BF16), v_ref[0, pl.ds(off, t), heads[h]],
                                  preferred_element_type=F32)
            later_scr[h] = later + totals[h]

    def body(jj, carry):
        tile(jj, diagonal=False)
        return carry

    tile(0, diagonal=True)
    lax.fori_loop(1, qi + 1, body, 0)
    for h in range(SB_HEADS):
        o_ref[0, :, h * SB_DIM:(h + 1) * SB_DIM] = acc_scr[h].astype(o_ref.dtype)


def _stick_breaking(proj3):
    bsz, seq, _ = proj3.shape
    t = SB_TILE
    bw = BRANCH_WIDTH
    base = (GATE_COLS + 5 * bw) // bw
    up = (jnp.arange(t)[:, None] > jnp.arange(t)[None, :]).astype(BF16)
    return pl.pallas_call(
        _sb_kernel,
        grid=(bsz, seq // t),
        in_specs=[pl.BlockSpec((1, t, bw), lambda b, i: (b, i, base)),
                  pl.BlockSpec((1, seq, bw), lambda b, i: (b, 0, base + 1)),
                  pl.BlockSpec((1, seq, bw), lambda b, i: (b, 0, base + 2)),
                  pl.BlockSpec((t, t), lambda b, i: (0, 0))],
        out_specs=pl.BlockSpec((1, t, bw), lambda b, i: (b, i, 0)),
        out_shape=jax.ShapeDtypeStruct((bsz, seq, bw), BF16),
        scratch_shapes=[pltpu.VMEM((SB_HEADS, t, SB_DIM), F32), pltpu.VMEM((SB_HEADS, t, 1), F32)],
        compiler_params=_cparams(("parallel", "arbitrary")),
        name="stick_breaking",
    )(proj3, proj3, proj3, up)


def _pool_kernel(p_ref, bc_ref, bp_ref, w_ref, sc_ref, o_ref, *, seq):
    g = pl.program_id(1)
    tile = LANE
    window = lax.shift_left(jnp.int32(2), g)
    row = lax.broadcasted_iota(jnp.int32, (tile, LANE), 0)
    band_cur = bc_ref[0]
    band_prev = bp_ref[0]
    wmix = w_ref[0]
    scale = sc_ref[0]

    group = min(4, seq // tile)

    def body(it, carry):
        offs, curs, wins = [], [], []
        for k in range(group):
            i = it * group + k
            off = pl.multiple_of(i * tile, tile)
            poff = pl.multiple_of(jnp.maximum(i - 1, 0) * tile, tile)
            cur = p_ref[0, pl.ds(off, tile), :]
            prev = p_ref[0, pl.ds(poff, tile), :]
            win = jnp.dot(band_cur, cur, preferred_element_type=F32)
            win = win + jnp.where(i > 0, jnp.dot(band_prev, prev, preferred_element_type=F32), 0.0)
            offs.append(off)
            curs.append(cur)
            wins.append(win)
        for k in range(group):
            count = jnp.minimum(row + (offs[k] + 1), window).astype(F32)
            pooled = wins[k] / count - curs[k].astype(F32)
            mixed = jnp.dot(pooled.astype(BF16), wmix, preferred_element_type=F32) * scale
            o_ref[0, pl.ds(offs[k], tile), :] = mixed.astype(o_ref.dtype)
        return carry

    lax.fori_loop(0, seq // (tile * group), body, 0)


def _pool(proj3, pool_w, pool_scale):
    bsz, seq, _ = proj3.shape
    ng = len(POOL_WINDOWS)
    base = PROJ_BLK0 + 8 * BRANCH_WIDTH // LANE
    t = jnp.arange(LANE)
    lag = t[:, None] - t[None, :]
    wins = jnp.asarray(POOL_WINDOWS)[:, None, None]
    band_cur = ((lag[None] >= 0) & (lag[None] < wins)).astype(BF16)
    band_prev = (((lag[None] + LANE) >= 0) & ((lag[None] + LANE) < wins)).astype(BF16)
    mat = pl.BlockSpec((1, LANE, LANE), lambda b, g: (g, 0, 0))
    return pl.pallas_call(
        functools.partial(_pool_kernel, seq=seq),
        grid=(bsz, ng),
        in_specs=[pl.BlockSpec((1, seq, LANE), lambda b, g: (b, 0, base + g)),
                  mat, mat, mat,
                  pl.BlockSpec((1, 1, LANE), lambda b, g: (g, 0, 0))],
        out_specs=pl.BlockSpec((1, seq, LANE), lambda b, g: (b, 0, g)),
        out_shape=jax.ShapeDtypeStruct((bsz, seq, BRANCH_WIDTH), BF16),
        compiler_params=_cparams(("parallel", "parallel")),
        name="pool",
    )(proj3, band_cur, band_prev, pool_w.astype(BF16), pool_scale.astype(F32).reshape(ng, 1, LANE))


def _merge_kernel(x_ref, g1_ref, s5_ref, hg_ref, sb_ref, po_ref, gt0_ref, gt1_ref, gt2_ref, gt3_ref,
                  wbr_ref, wout_ref, o_ref):
    dot = functools.partial(jnp.dot, preferred_element_type=F32)
    branches = (s5_ref, hg_ref, sb_ref, po_ref)
    gates = (gt0_ref, gt1_ref, gt2_ref, gt3_ref)
    merged = None
    for n in range(N_BRANCH):
        term = gates[n][...].astype(F32) * dot(branches[n][...], wbr_ref[n])
        merged = term if merged is None else merged + term
    mix = dot(merged.astype(BF16), wout_ref[...])
    o_ref[...] = x_ref[...] + g1_ref[0] * mix


def _merge(x2d, g1, s5o, hg, sb, po, proj, w_branch, w_out, seq):
    t, d = x2d.shape
    bw = BRANCH_WIDTH
    tm = min(256, seq)
    per_b = seq // tm
    tok = lambda w: pl.BlockSpec((tm, w), lambda i: (i, 0))
    gate = lambda n: pl.BlockSpec((tm, d), lambda i, n=n: (i, n))
    const = lambda shape: pl.BlockSpec(shape, lambda i: (0,) * len(shape))
    return pl.pallas_call(
        _merge_kernel,
        grid=(t // tm,),
        in_specs=[tok(d), pl.BlockSpec((1, 1, d), lambda i: (i // per_b, 0, 0)),
                  tok(bw), tok(bw), tok(bw), tok(bw),
                  gate(0), gate(1), gate(2), gate(3),
                  const((N_BRANCH, bw, d)), const((d, d))],
        out_specs=tok(d),
        out_shape=jax.ShapeDtypeStruct((t, d), F32),
        compiler_params=_cparams(("parallel",)),
        name="merge",
    )(x2d, g1, s5o, hg, sb, po, proj, proj, proj, proj, w_branch.astype(BF16), w_out.astype(BF16))


def _top_values(s, n):
    vals = []
    for r in range(n):
        cur = s if r == 0 else jnp.where(s < vals[-1], s, NEG_BIG)
        vals.append(jnp.max(cur, axis=0, keepdims=True))
    return vals


def _prefix_length(rows, pred):
    c1 = pred(rows[7])
    c2 = pred(jnp.where(c1, rows[11], rows[3]))
    c3 = pred(jnp.where(c1, jnp.where(c2, rows[13], rows[9]), jnp.where(c2, rows[5], rows[1])))
    pick = lambda a: jnp.where(c3, rows[a + 2], rows[a])
    c4 = pred(jnp.where(c1, jnp.where(c2, pick(12), pick(8)), jnp.where(c2, pick(4), pick(0))))
    c5 = pred(rows[15])
    one = lambda c, v: jnp.where(c, v, 0.0)
    return (one(c1, 8.0) + one(c2, 4.0)) + (one(c3, 2.0) + one(c4, 1.0)) + one(c5, 1.0)


def _route_kernel(x_ref, gain_ref, sh_ref, sc_ref, wq_ref, keys_ref,
                  ht_ref, cnt_ref, e1_ref, rank_ref, e2_ref):
    k = PEER_TOPK
    h = _rms_modulate(x_ref[...], gain_ref[...], sh_ref[0], sc_ref[0])
    ht = h.T.astype(BF16)
    ht_ref[...] = ht
    qt = jnp.dot(wq_ref[...], ht, preferred_element_type=F32)
    for hd in range(PEER_HEADS):
        sc = []
        for half in range(2):
            lo = (2 * hd + half) * PEER_NKEYS
            qs = qt[lo:lo + PEER_NKEYS, :].astype(BF16)
            sc.append(jnp.dot(keys_ref[hd, half], qs, preferred_element_type=F32))
        top1 = _top_values(sc[0], k)
        top2 = _top_values(sc[1], k)
        v1 = jnp.concatenate(top1, axis=0)
        v2 = jnp.concatenate(top2, axis=0)
        cand = [top1[0] + v2] + [top1[a] + v2[:8] for a in range(1, 8)] + [v1[8:] + top2[0]]
        best = _top_values(jnp.concatenate(cand, axis=0), k)
        tau = best[k - 1]
        z = None
        for r in range(k):
            e = jnp.exp(best[r] - best[0])
            z = e if z is None else z + e
        s1, s2 = sc
        cnt_ref[hd] = _prefix_length(top2, lambda row: s1 + row >= tau)
        rank_ref[hd] = _prefix_length(top2, lambda row: row > s2).astype(BF16)
        e1_ref[hd] = jnp.exp(s1 - top1[0]) / z
        e2_ref[hd] = jnp.exp(s2 - top2[0]).astype(BF16)


def _route(x2d, gain, shift, scale, wq_t, keys, seq):
    t, d = x2d.shape
    tm = min(256, seq)
    per_b = seq // tm
    nh, nk = PEER_HEADS, PEER_NKEYS
    big = pl.BlockSpec((nh, nk, tm), lambda i: (0, 0, i))
    big_f32 = jax.ShapeDtypeStruct((nh, nk, t), F32)
    big_bf16 = jax.ShapeDtypeStruct((nh, nk, t), BF16)
    return pl.pallas_call(
        _route_kernel,
        grid=(t // tm,),
        in_specs=[pl.BlockSpec((tm, d), lambda i: (i, 0)),
                  pl.BlockSpec((1, d), lambda i: (0, 0)),
                  pl.BlockSpec((1, 1, d), lambda i: (i // per_b, 0, 0)),
                  pl.BlockSpec((1, 1, d), lambda i: (i // per_b, 0, 0)),
                  pl.BlockSpec(wq_t.shape, lambda i: (0, 0)),
                  pl.BlockSpec(keys.shape, lambda i: (0, 0, 0, 0))],
        out_specs=[pl.BlockSpec((d, tm), lambda i: (0, i)), big, big, big, big],
        out_shape=[jax.ShapeDtypeStruct((d, t), BF16), big_f32, big_f32, big_bf16, big_bf16],
        compiler_params=_cparams(("parallel",)),
        name="peer_route",
    )(x2d, gain.reshape(1, d), shift, scale, wq_t, keys)


def _rows_bf16(row, n):
    tile = jnp.broadcast_to(row, (16, row.shape[1])).astype(BF16)
    return jnp.concatenate([tile] * (n // 16), axis=0)


def _peer_kernel(ht_ref, u_ref, vt_ref, cnt_ref, e1_ref, rank_ref, e2_ref, x_ref, g2_ref, fg_ref,
                 o_ref, acc_ref, *, te, tm, final_norm):
    e = pl.program_id(1)
    nk = PEER_NKEYS
    blk = PEER_BLOCK
    rows = blk // nk
    dot = functools.partial(jnp.dot, preferred_element_type=F32)

    @pl.when(e == 0)
    def _():
        acc_ref[...] = jnp.zeros_like(acc_ref)

    nblk = te // blk

    def first(c):
        return dot(u_ref[c * blk:(c + 1) * blk, :], ht_ref[...])

    def gated(c, act):
        parts = []
        for ii in range(rows):
            i = c * rows + ii
            gate = jnp.zeros((nk, tm), BF16)
            for hd in range(PEER_HEADS):
                count = _rows_bf16(cnt_ref[hd, i:i + 1, :], nk)
                e1 = _rows_bf16(e1_ref[hd, i:i + 1, :], nk)
                gate = gate + jnp.where(rank_ref[hd] < count, e1 * e2_ref[hd], jnp.zeros((), BF16))
            parts.append(gate * _gelu(act[ii * nk:(ii + 1) * nk, :]).astype(BF16))
        return jnp.concatenate(parts, axis=0)

    ahead = 2
    acts = {c: first(c) for c in range(min(ahead, nblk))}
    for c in range(nblk):
        p = gated(c, acts.pop(c))
        if c + ahead < nblk:
            acts[c + ahead] = first(c + ahead)
        acc_ref[...] += dot(vt_ref[:, c * blk:(c + 1) * blk], p)

    @pl.when(e == pl.num_programs(1) - 1)
    def _():
        y = x_ref[...] + g2_ref[0] * acc_ref[...].T
        if final_norm:
            y = y * lax.rsqrt(jnp.mean(y * y, axis=-1, keepdims=True) + EPS) * fg_ref[...]
        o_ref[...] = y


def _peer(ht, u_bf, v_bf, cnt, e1, rank2, e2, x2d, g2, final_gain, final_norm, seq):
    t, d = x2d.shape
    ne = u_bf.shape[0]
    tm = min(512, seq)
    te = 1024
    per_b = seq // tm
    nh, nk = PEER_HEADS, PEER_NKEYS
    big = pl.BlockSpec((nh, nk, tm), lambda i, g: (0, 0, i))
    rows = pl.BlockSpec((nh, te // nk, tm), lambda i, g: (0, g, i))
    return pl.pallas_call(
        functools.partial(_peer_kernel, te=te, tm=tm, final_norm=final_norm),
        grid=(t // tm, ne // te),
        in_specs=[pl.BlockSpec((d, tm), lambda i, g: (0, i)),
                  pl.BlockSpec((te, d), lambda i, g: (g, 0)),
                  pl.BlockSpec((d, te), lambda i, g: (0, g)),
                  rows, rows, big, big,
                  pl.BlockSpec((tm, d), lambda i, g: (i, 0)),
                  pl.BlockSpec((1, 1, d), lambda i, g: (i // per_b, 0, 0)),
                  pl.BlockSpec((1, d), lambda i, g: (0, 0))],
        out_specs=pl.BlockSpec((tm, d), lambda i, g: (i, 0)),
        out_shape=jax.ShapeDtypeStruct((t, d), F32),
        scratch_shapes=[pltpu.VMEM((d, tm), F32)],
        compiler_params=_cparams(("parallel", "arbitrary"), vmem=PEER_VMEM_LIMIT),
        name="peer_experts",
    )(ht, u_bf, v_bf.T, cnt, e1, rank2, e2, x2d, g2, final_gain.astype(F32).reshape(1, d))


def _layer(x2d, c, bsz, seq, ada_w, ada_b, norm_mix_gain, norm_ffn_gain, w_in, s5_params, s5_d,
           s5_w_glu, s5_b_glu, lower_bound, hg_norm_gain, pool_w, pool_scale, w_gate, w_branch, w_out,
           peer_w_query, peer_sub_keys, peer_u, peer_v, final_gain, last):
    d = D_MODEL
    mod = _ada(c.astype(F32), ada_w.astype(F32), ada_b.astype(F32))
    sh1, sc1, g1, sh2, sc2, g2 = [m.reshape(bsz, 1, d) for m in jnp.split(mod, 6, axis=-1)]
    wcat = jnp.concatenate([w_gate[n] for n in range(N_BRANCH)] + [w_in], axis=1).astype(BF16)
    proj = _inproj(x2d, norm_mix_gain.astype(F32), sh1, sc1, wcat, seq)
    proj3 = proj.reshape(bsz, seq, CAT_COLS)
    bbd, cbd, tabs = _s5_tables(*s5_params)
    s5o = _s5(proj3, bbd, cbd, tabs, s5_d, s5_w_glu, s5_b_glu)
    hg = _hgrn(proj3, lower_bound, hg_norm_gain.astype(F32))
    sb = _stick_breaking(proj3)
    po = _pool(proj3, pool_w, pool_scale)
    t = bsz * seq
    flat = lambda a: a.reshape(t, BRANCH_WIDTH)
    x1 = _merge(x2d, g1, flat(s5o), flat(hg), flat(sb), flat(po), proj, w_branch, w_out, seq)
    ht, cnt, e1, rank2, e2 = _route(x1, norm_ffn_gain.astype(F32), sh2, sc2,
                                    peer_w_query.T.astype(BF16), peer_sub_keys.astype(BF16), seq)
    return _peer(ht, peer_u.astype(BF16), peer_v.astype(BF16), cnt, e1, rank2, e2, x1, g2, final_gain, last,
                 seq)


def kernel(x, c, ada_w, ada_b, norm_mix_gain, norm_ffn_gain, w_in, s5_lambda_re, s5_lambda_im, s5_log_step, s5_b_re, s5_b_im, s5_c_re, s5_c_im, s5_d, s5_w_glu, s5_b_glu, hg_lb_logits, hg_norm_gain, pool_w, pool_scale, w_gate, w_branch, w_out, peer_w_query, peer_sub_keys, peer_u, peer_v, final_gain):
    bsz, seq, d = x.shape
    depth = ada_w.shape[0]
    lb_soft = jax.nn.softmax(hg_lb_logits.astype(F32), axis=0)
    lower_bounds = jnp.cumsum(lb_soft, axis=0) - lb_soft[0:1]
    x2d = x.astype(F32).reshape(bsz * seq, d)
    for l in range(depth):
        s5_params = (s5_lambda_re[l], s5_lambda_im[l], s5_log_step[l], s5_b_re[l], s5_b_im[l],
                     s5_c_re[l], s5_c_im[l])
        x2d = _layer(x2d, c, bsz, seq, ada_w[l], ada_b[l], norm_mix_gain[l], norm_ffn_gain[l], w_in[l],
                     s5_params, s5_d[l], s5_w_glu[l], s5_b_glu[l], lower_bounds[l], hg_norm_gain[l],
                     pool_w[l], pool_scale[l], w_gate[l], w_branch[l], w_out[l], peer_w_query[l],
                     peer_sub_keys[l], peer_u[l], peer_v[l], final_gain, l == depth - 1)
    return x2d.reshape(bsz, seq, d).astype(x.dtype)
```

```python
import functools
import math

import jax
import jax.numpy as jnp
from jax import lax
from jax.experimental import pallas as pl
from jax.experimental.pallas import tpu as pltpu

F32 = jnp.float32
BF16 = jnp.bfloat16

D_MODEL = 2048
BRANCH_WIDTH = 512
N_BRANCH = 4
LANE = 128
S5_GROUP = 16
S5_GROUPS = 32
S5_STATE = 64
S5_MIN_NEG = 1e-4
S5_ROWS = 8
HG_HEADS = 4
HG_F_MIN = 1e-6
HG_K_MIN = 1e-30
HG_CHUNK = 128
HG_SUB = 16
HG_STEP_HEADS = 4
SB_HEADS = 4
SB_DIM = 128
SB_TILE = 256
POOL_WINDOWS = (2, 4, 8, 16)
PEER_HEADS = 8
PEER_NKEYS = 128
PEER_TOPK = 16
PEER_BLOCK = 256
EPS = 1e-6
NEG_BIG = -3.0e38

GATE_COLS = N_BRANCH * D_MODEL
IN_COLS = 9 * BRANCH_WIDTH
CAT_COLS = GATE_COLS + IN_COLS
PROJ_BLK0 = GATE_COLS // LANE
INPROJ_CHUNK = 256
VMEM_LIMIT = 48 * 1024 * 1024
PEER_VMEM_LIMIT = 56 * 1024 * 1024


def _cparams(semantics, vmem=VMEM_LIMIT):
    return pltpu.CompilerParams(dimension_semantics=semantics, vmem_limit_bytes=vmem)


def _split3(x):
    hi = x.astype(BF16)
    r1 = x - hi.astype(F32)
    mid = r1.astype(BF16)
    lo = (r1 - mid.astype(F32)).astype(BF16)
    return hi, mid, lo


def _dot01_left(m01, x):
    hi, mid, lo = _split3(x)
    d = functools.partial(jnp.dot, preferred_element_type=F32)
    return d(m01, hi) + d(m01, mid) + d(m01, lo)


def _dot01_right2(x, m01):
    hi = x.astype(BF16)
    lo = (x - hi.astype(F32)).astype(BF16)
    d = functools.partial(jnp.dot, preferred_element_type=F32)
    return d(hi, m01) + d(lo, m01)


def _dot_f32(a, b):
    ah, am, al = _split3(a)
    bh, bm, bl = _split3(b)
    d = functools.partial(jnp.dot, preferred_element_type=F32)
    return (d(ah, bh) + (d(ah, bm) + d(am, bh))) + ((d(ah, bl) + d(al, bh)) + d(am, bm))


def _gelu(x):
    return 0.5 * x * (1.0 + lax.erf(x * (1.0 / math.sqrt(2.0))))


def _sigmoid(x):
    return 0.5 * jnp.tanh(0.5 * x) + 0.5


def _silu(x):
    return x * _sigmoid(x)


def _rms_modulate(x, gain, shift, scale):
    y = x * lax.rsqrt(jnp.mean(x * x, axis=-1, keepdims=True) + EPS) * gain
    return y * (1.0 + scale) + shift


def _ada_kernel(c_ref, w_ref, b_ref, o_ref):
    o_ref[...] = _dot_f32(_silu(c_ref[...]), w_ref[...]) + b_ref[...]


def _ada(c, w, b):
    bsz, d = c.shape
    n = w.shape[1]
    tn = 1024
    return pl.pallas_call(
        _ada_kernel,
        grid=(n // tn,),
        in_specs=[pl.BlockSpec((bsz, d), lambda j: (0, 0)),
                  pl.BlockSpec((d, tn), lambda j: (0, j)),
                  pl.BlockSpec((1, tn), lambda j: (0, j))],
        out_specs=pl.BlockSpec((bsz, tn), lambda j: (0, j)),
        out_shape=jax.ShapeDtypeStruct((bsz, n), F32),
        compiler_params=_cparams(("arbitrary",)),
        name="ada_mod",
    )(c, w, b.reshape(1, n))


def _inproj_kernel(x_ref, gain_ref, sh_ref, sc_ref, w_ref, o_ref, h_ref, *, gate_chunks):
    j = pl.program_id(1)

    @pl.when(j == 0)
    def _():
        def rows(r, c):
            sl = pl.ds(pl.multiple_of(r * LANE, LANE), LANE)
            h_ref[sl, :] = _rms_modulate(x_ref[sl, :], gain_ref[...], sh_ref[0], sc_ref[0]).astype(BF16)
            return c

        lax.fori_loop(0, x_ref.shape[0] // LANE, rows, 0)

    cw = INPROJ_CHUNK
    nch = w_ref.shape[1] // cw
    mm = lambda k: jnp.dot(h_ref[...], w_ref[:, k * cw:(k + 1) * cw], preferred_element_type=F32)
    parts = {0: mm(0), 1: mm(1)}
    for k in range(nch):
        r = parts.pop(k)
        if k + 2 < nch:
            parts[k + 2] = mm(k + 2)
        is_gate = j * nch + k < gate_chunks
        o_ref[:, k * cw:(k + 1) * cw] = jnp.where(is_gate, _sigmoid(r), r).astype(o_ref.dtype)


def _inproj(x2d, gain, shift, scale, wcat, seq):
    t, d = x2d.shape
    n = wcat.shape[1]
    tm = min(1024, seq)
    tn = 5 * INPROJ_CHUNK
    per_b = seq // tm
    return pl.pallas_call(
        functools.partial(_inproj_kernel, gate_chunks=GATE_COLS // INPROJ_CHUNK),
        grid=(t // tm, n // tn),
        in_specs=[pl.BlockSpec((tm, d), lambda i, j: (i, 0)),
                  pl.BlockSpec((1, d), lambda i, j: (0, 0)),
                  pl.BlockSpec((1, 1, d), lambda i, j: (i // per_b, 0, 0)),
                  pl.BlockSpec((1, 1, d), lambda i, j: (i // per_b, 0, 0)),
                  pl.BlockSpec((d, tn), lambda i, j: (0, j))],
        out_specs=pl.BlockSpec((tm, tn), lambda i, j: (i, j)),
        out_shape=jax.ShapeDtypeStruct((t, n), BF16),
        scratch_shapes=[pltpu.VMEM((tm, d), BF16)],
        compiler_params=_cparams(("parallel", "arbitrary")),
        name="inproj",
    )(x2d, gain.reshape(1, d), shift, scale, wcat)


def _s5_tables(lam_re, lam_im, log_step, b_re, b_im, c_re, c_im):
    g, p, h = S5_GROUPS, S5_STATE, S5_GROUP
    lr = jnp.minimum(lam_re.astype(F32), -S5_MIN_NEG)
    li = lam_im.astype(F32)
    step = jnp.exp(log_step.astype(F32))[:, None]
    mag = jnp.exp(lr * step)
    ab_re = mag * jnp.cos(li * step)
    ab_im = mag * jnp.sin(li * step)
    den = lr * lr + li * li
    nr = ab_re - 1.0
    fr = (nr * lr + ab_im * li) / den
    fi = (ab_im * lr - nr * li) / den
    br, bi = b_re.astype(F32), b_im.astype(F32)
    bb_re = fr[..., None] * br - fi[..., None] * bi
    bb_im = fr[..., None] * bi + fi[..., None] * br
    eye_g = jnp.eye(g, dtype=F32)

    def in_map(m):
        return jnp.einsum('gph,gk->ghkp', m, eye_g).reshape(g * h, g * p)

    def out_map(m):
        return jnp.einsum('ghp,gk->gpkh', m, eye_g).reshape(g * p, g * h)

    bbd = jnp.concatenate([in_map(bb_re), in_map(bb_im)], axis=1).astype(BF16)
    cbd = jnp.concatenate([out_map(c_re.astype(F32)), out_map(-c_im.astype(F32))], axis=0).astype(BF16)
    prs, pis = [ab_re], [ab_im]
    for _ in range(S5_ROWS - 1):
        pr_, pi_ = prs[-1], pis[-1]
        prs.append(pr_ * ab_re - pi_ * ab_im)
        pis.append(pr_ * ab_im + pi_ * ab_re)
    rows = jnp.arange(S5_ROWS)[:, None]
    tabs = []
    for d in (1, 2, 4):
        tabs.append(jnp.where(rows >= d, prs[d - 1].reshape(1, g * p), 0.0))
        tabs.append(jnp.where(rows >= d, pis[d - 1].reshape(1, g * p), 0.0))
    tabs.append(jnp.stack(prs).reshape(S5_ROWS, g * p))
    tabs.append(jnp.stack(pis).reshape(S5_ROWS, g * p))
    return bbd, cbd, jnp.stack(tabs).astype(F32)


def _s5_kernel(u_ref, bbd_ref, tab_ref, cbd_ref, d_ref, wglu_ref, bglu_ref, o_ref,
               bu_scr, x_scr, carry_scr, *, tile):
    ns = S5_GROUPS * S5_STATE
    lc = 512

    @pl.when(pl.program_id(1) == 0)
    def _():
        carry_scr[...] = jnp.zeros_like(carry_scr)

    u = u_ref[0]
    dot = functools.partial(jnp.dot, preferred_element_type=F32)
    gl = lc // (S5_STATE // S5_GROUP)
    nch = ns // lc

    def project(ch):
        ub = u[:, ch * gl:(ch + 1) * gl]
        for part in (0, ns):
            cols = slice(part + ch * lc, part + (ch + 1) * lc)
            bu_scr[:, cols] = dot(ub, bbd_ref[ch * gl:(ch + 1) * gl, cols])

    def scan(ch):
        re_l = slice(ch * lc, (ch + 1) * lc)
        im_l = slice(ns + ch * lc, ns + (ch + 1) * lc)
        cr = carry_scr[:, re_l]
        ci = carry_scr[:, im_l]
        for r in range(tile // S5_ROWS):
            rows = slice(r * S5_ROWS, (r + 1) * S5_ROWS)
            xr = bu_scr[rows, re_l]
            xi = bu_scr[rows, im_l]
            for k, d in enumerate((1, 2, 4)):
                sr = pltpu.roll(xr, d, 0)
                si = pltpu.roll(xi, d, 0)
                mr = tab_ref[2 * k, :, re_l]
                mi = tab_ref[2 * k + 1, :, re_l]
                xr, xi = xr + (mr * sr - mi * si), xi + (mr * si + mi * sr)
            ar = tab_ref[6, :, re_l]
            ai = tab_ref[7, :, re_l]
            xr, xi = xr + (ar * cr - ai * ci), xi + (ar * ci + ai * cr)
            x_scr[rows, re_l] = xr
            x_scr[rows, im_l] = xi
            cr = jnp.broadcast_to(xr[S5_ROWS - 1:S5_ROWS], (S5_ROWS, lc))
            ci = jnp.broadcast_to(xi[S5_ROWS - 1:S5_ROWS], (S5_ROWS, lc))
        carry_scr[:, re_l] = cr
        carry_scr[:, im_l] = ci

    def readout(ch):
        out_l = slice(ch * gl, (ch + 1) * gl)
        return sum(dot(x_scr[:, part + ch * lc:part + (ch + 1) * lc].astype(BF16),
                       cbd_ref[part + ch * lc:part + (ch + 1) * lc, out_l]) for part in (0, ns))

    project(0)
    ys = []
    for ch in range(nch):
        if ch + 1 < nch:
            project(ch + 1)
        scan(ch)
        ys.append(readout(ch))
    y = _gelu(jnp.concatenate(ys, axis=1) + d_ref[...] * u.astype(F32))
    gate = _sigmoid(dot(y.astype(BF16), wglu_ref[...]) + bglu_ref[...])
    o_ref[0] = (y * gate).astype(o_ref.dtype)


def _s5(proj3, bbd, cbd, tabs, d_skip, w_glu, b_glu):
    bsz, seq, _ = proj3.shape
    bw = BRANCH_WIDTH
    ns = S5_GROUPS * S5_STATE
    tile = min(256, seq)
    const = lambda shape: pl.BlockSpec(shape, lambda b, i: (0,) * len(shape))
    return pl.pallas_call(
        functools.partial(_s5_kernel, tile=tile),
        grid=(bsz, seq // tile),
        in_specs=[pl.BlockSpec((1, tile, bw), lambda b, i: (b, i, GATE_COLS // bw)),
                  const((bw, 2 * ns)), const((8, S5_ROWS, ns)), const((2 * ns, bw)),
                  const((1, bw)), const((bw, bw)), const((1, bw))],
        out_specs=pl.BlockSpec((1, tile, bw), lambda b, i: (b, i, 0)),
        out_shape=jax.ShapeDtypeStruct((bsz, seq, bw), BF16),
        scratch_shapes=[pltpu.VMEM((tile, 2 * ns), F32), pltpu.VMEM((tile, 2 * ns), F32),
                        pltpu.VMEM((S5_ROWS, 2 * ns), F32)],
        compiler_params=_cparams(("parallel", "arbitrary")),
        name="s5",
    )(proj3, bbd, tabs, cbd, d_skip.astype(F32).reshape(1, bw), w_glu.astype(BF16),
      b_glu.astype(F32).reshape(1, bw))


def _hgrn_kernel(q_ref, f_ref, i_ref, g_ref, lb_ref, gain_ref, tril_ref, o_ref, st_scr, *, seq):
    c, sub = HG_CHUNK, HG_SUB
    tril = tril_ref[...]
    row = lax.broadcasted_iota(jnp.int32, (c, LANE), 0)
    rsub = jnp.bitwise_and(row, sub - 1)
    tcol = lax.broadcasted_iota(jnp.int32, (sub, c), 1)
    st_scr[...] = jnp.zeros_like(st_scr)
    nt = (((1,), (1,)), ((), ()))

    def head_chunk(off, hh):
        hl = slice(hh * LANE, (hh + 1) * LANE)
        lb = lb_ref[0, :, hl]
        gain = gain_ref[0, :, hl]
        q = _silu(q_ref[0, pl.ds(off, c), hl].astype(F32))
        fg = lb + (1.0 - lb) * _sigmoid(f_ref[0, pl.ds(off, c), hl].astype(F32))
        logf = jnp.log(jnp.maximum(fg, HG_F_MIN))
        k = 1.0 - fg
        v = i_ref[0, pl.ds(off, c), hl]
        vf = v.astype(F32)
        bc = _dot01_left(tril, logf)
        bk = bc - jnp.log(jnp.maximum(k, HG_K_MIN))
        o = jnp.zeros((c, LANE), F32)
        for lag in range(sub):
            if lag == 0:
                bd, vd = bk, vf
            else:
                bd = pltpu.roll(bk, lag, 0)
                vd = pltpu.roll(vf, lag, 0)
            valid = rsub >= lag
            e = jnp.exp(jnp.where(valid, bc - bd, 0.0))
            w = jnp.sum(q * e, axis=-1, keepdims=True)
            o = o + jnp.where(valid, w * vd, 0.0)
        parts = []
        for j in range(c // sub - 1):
            lo, hi = sub * j, sub * (j + 1)
            bend = bc[hi - 1:hi, :]
            qj = (q * jnp.exp(jnp.minimum(bc - bend, 0.0))).astype(BF16)
            kj = (k[lo:hi] * jnp.exp(bend - bc[lo:hi])).astype(BF16)
            st = lax.dot_general(kj, qj, nt, preferred_element_type=F32)
            parts.append(jnp.where(tcol >= hi, st, 0.0))
        parts.append(jnp.zeros((sub, c), F32))
        scores = jnp.concatenate(parts, axis=0).T
        o = o + jnp.dot(scores.astype(BF16), v, preferred_element_type=F32)
        st_prev = st_scr[hh]
        qe = (q * jnp.exp(bc)).astype(BF16)
        o = o + lax.dot_general(qe, st_prev.astype(BF16), nt, preferred_element_type=F32)
        blast = bc[c - 1:c, :]
        kdec = (k * jnp.exp(blast - bc)).astype(BF16)
        st_scr[hh] = st_prev * jnp.exp(blast) + jnp.dot(vf.T.astype(BF16), kdec,
                                                        preferred_element_type=F32)
        o = o * lax.rsqrt(jnp.mean(o * o, axis=-1, keepdims=True) + EPS) * gain
        o = o * _silu(g_ref[0, pl.ds(off, c), hl].astype(F32))
        o_ref[0, pl.ds(off, c), hl] = o.astype(o_ref.dtype)

    def chunk(ci, carry):
        off = pl.multiple_of(ci * c, c)
        for hh in range(HG_STEP_HEADS):
            head_chunk(off, hh)
        return carry

    lax.fori_loop(0, seq // c, chunk, 0)


def _hgrn(proj3, lower_bound, norm_gain):
    bsz, seq, _ = proj3.shape
    c = HG_CHUNK
    tril = jnp.tril(jnp.ones((c, c), F32)).astype(BF16)
    base = PROJ_BLK0 + BRANCH_WIDTH // LANE

    pw = HG_STEP_HEADS * LANE

    def col(k):
        return pl.BlockSpec((1, seq, pw), lambda b, h: (b, 0, (base + k * HG_HEADS) // HG_STEP_HEADS + h))

    vec = pl.BlockSpec((1, 1, pw), lambda b, h: (h, 0, 0))
    return pl.pallas_call(
        functools.partial(_hgrn_kernel, seq=seq),
        grid=(bsz, HG_HEADS // HG_STEP_HEADS),
        in_specs=[col(0), col(1), col(2), col(3), vec, vec,
                  pl.BlockSpec((c, c), lambda b, h: (0, 0))],
        out_specs=pl.BlockSpec((1, seq, pw), lambda b, h: (b, 0, h)),
        out_shape=jax.ShapeDtypeStruct((bsz, seq, BRANCH_WIDTH), BF16),
        scratch_shapes=[pltpu.VMEM((HG_STEP_HEADS, LANE, LANE), F32)],
        compiler_params=_cparams(("parallel", "parallel")),
        name="hgrn2",
    )(proj3, proj3, proj3, proj3, lower_bound.reshape(HG_HEADS // HG_STEP_HEADS, 1, pw),
      norm_gain.reshape(HG_HEADS // HG_STEP_HEADS, 1, pw), tril)


def _sb_kernel(q_ref, k_ref, v_ref, up_ref, o_ref, acc_scr, later_scr):
    qi = pl.program_id(1)
    t = SB_TILE
    up = up_ref[...]
    scale = 1.0 / math.sqrt(SB_DIM)
    row = lax.broadcasted_iota(jnp.int32, (t, t), 0)
    colv = lax.broadcasted_iota(jnp.int32, (t, t), 1)
    below = colv < row
    nt = (((1,), (1,)), ((), ()))
    acc_scr[...] = jnp.zeros_like(acc_scr)
    later_scr[...] = jnp.zeros_like(later_scr)

    def tile(jj, diagonal):
        off = pl.multiple_of((qi - jj) * t, t)
        keep = (lambda a: jnp.where(below, a, 0.0)) if diagonal else (lambda a: a)
        heads = [slice(h * SB_DIM, (h + 1) * SB_DIM) for h in range(SB_HEADS)]
        zs = [lax.dot_general(q_ref[0, :, hl], k_ref[0, pl.ds(off, t), hl], nt,
                              preferred_element_type=F32) * scale for hl in heads]
        logits, suffixes, totals = [], [], []
        for h in range(SB_HEADS):
            z = zs[h]
            sp = jnp.maximum(z, 0.0) + jnp.log(1.0 + jnp.exp(-jnp.abs(z)))
            log_not = keep(-sp)
            suffixes.append(_dot01_right2(log_not, up))
            logits.append(z - sp)
            totals.append(jnp.sum(log_not, axis=-1, keepdims=True))
        for h in range(SB_HEADS):
            later = later_scr[h]
            w = keep(jnp.exp(logits[h] + suffixes[h] + later))
            acc_scr[h] += jnp.dot(w.astype(BF16), v_ref[0, pl.ds(off, t), heads[h]],
                                  preferred_element_type=F32)
            later_scr[h] = later + totals[h]

    def body(jj, carry):
        tile(jj, diagonal=False)
        return carry

    tile(0, diagonal=True)
    lax.fori_loop(1, qi + 1, body, 0)
    for h in range(SB_HEADS):
        o_ref[0, :, h * SB_DIM:(h + 1) * SB_DIM] = acc_scr[h].astype(o_ref.dtype)


def _stick_breaking(proj3):
    bsz, seq, _ = proj3.shape
    t = SB_TILE
    bw = BRANCH_WIDTH
    base = (GATE_COLS + 5 * bw) // bw
    up = (jnp.arange(t)[:, None] > jnp.arange(t)[None, :]).astype(BF16)
    return pl.pallas_call(
        _sb_kernel,
        grid=(bsz, seq // t),
        in_specs=[pl.BlockSpec((1, t, bw), lambda b, i: (b, i, base)),
                  pl.BlockSpec((1, seq, bw), lambda b, i: (b, 0, base + 1)),
                  pl.BlockSpec((1, seq, bw), lambda b, i: (b, 0, base + 2)),
                  pl.BlockSpec((t, t), lambda b, i: (0, 0))],
        out_specs=pl.BlockSpec((1, t, bw), lambda b, i: (b, i, 0)),
        out_shape=jax.ShapeDtypeStruct((bsz, seq, bw), BF16),
        scratch_shapes=[pltpu.VMEM((SB_HEADS, t, SB_DIM), F32), pltpu.VMEM((SB_HEADS, t, 1), F32)],
        compiler_params=_cparams(("parallel", "arbitrary")),
        name="stick_breaking",
    )(proj3, proj3, proj3, up)


def _pool_kernel(p_ref, bc_ref, bp_ref, w_ref, sc_ref, o_ref, *, seq):
    g = pl.program_id(1)
    tile = LANE
    window = lax.shift_left(jnp.int32(2), g)
    row = lax.broadcasted_iota(jnp.int32, (tile, LANE), 0)
    band_cur = bc_ref[0]
    band_prev = bp_ref[0]
    wmix = w_ref[0]
    scale = sc_ref[0]

    group = min(4, seq // tile)

    def body(it, carry):
        offs, curs, wins = [], [], []
        for k in range(group):
            i = it * group + k
            off = pl.multiple_of(i * tile, tile)
            poff = pl.multiple_of(jnp.maximum(i - 1, 0) * tile, tile)
            cur = p_ref[0, pl.ds(off, tile), :]
            prev = p_ref[0, pl.ds(poff, tile), :]
            win = jnp.dot(band_cur, cur, preferred_element_type=F32)
            win = win + jnp.where(i > 0, jnp.dot(band_prev, prev, preferred_element_type=F32), 0.0)
            offs.append(off)
            curs.append(cur)
            wins.append(win)
        for k in range(group):
            count = jnp.minimum(row + (offs[k] + 1), window).astype(F32)
            pooled = wins[k] / count - curs[k].astype(F32)
            mixed = jnp.dot(pooled.astype(BF16), wmix, preferred_element_type=F32) * scale
            o_ref[0, pl.ds(offs[k], tile), :] = mixed.astype(o_ref.dtype)
        return carry

    lax.fori_loop(0, seq // (tile * group), body, 0)


def _pool(proj3, pool_w, pool_scale):
    bsz, seq, _ = proj3.shape
    ng = len(POOL_WINDOWS)
    base = PROJ_BLK0 + 8 * BRANCH_WIDTH // LANE
    t = jnp.arange(LANE)
    lag = t[:, None] - t[None, :]
    wins = jnp.asarray(POOL_WINDOWS)[:, None, None]
    band_cur = ((lag[None] >= 0) & (lag[None] < wins)).astype(BF16)
    band_prev = (((lag[None] + LANE) >= 0) & ((lag[None] + LANE) < wins)).astype(BF16)
    mat = pl.BlockSpec((1, LANE, LANE), lambda b, g: (g, 0, 0))
    return pl.pallas_call(
        functools.partial(_pool_kernel, seq=seq),
        grid=(bsz, ng),
        in_specs=[pl.BlockSpec((1, seq, LANE), lambda b, g: (b, 0, base + g)),
                  mat, mat, mat,
                  pl.BlockSpec((1, 1, LANE), lambda b, g: (g, 0, 0))],
        out_specs=pl.BlockSpec((1, seq, LANE), lambda b, g: (b, 0, g)),
        out_shape=jax.ShapeDtypeStruct((bsz, seq, BRANCH_WIDTH), BF16),
        compiler_params=_cparams(("parallel", "parallel")),
        name="pool",
    )(proj3, band_cur, band_prev, pool_w.astype(BF16), pool_scale.astype(F32).reshape(ng, 1, LANE))


def _merge_kernel(x_ref, g1_ref, s5_ref, hg_ref, sb_ref, po_ref, gt0_ref, gt1_ref, gt2_ref, gt3_ref,
                  wbr_ref, wout_ref, o_ref):
    dot = functools.partial(jnp.dot, preferred_element_type=F32)
    branches = (s5_ref, hg_ref, sb_ref, po_ref)
    gates = (gt0_ref, gt1_ref, gt2_ref, gt3_ref)
    merged = None
    for n in range(N_BRANCH):
        term = gates[n][...].astype(F32) * dot(branches[n][...], wbr_ref[n])
        merged = term if merged is None else merged + term
    mix = dot(merged.astype(BF16), wout_ref[...])
    o_ref[...] = x_ref[...] + g1_ref[0] * mix


def _merge(x2d, g1, s5o, hg, sb, po, proj, w_branch, w_out, seq):
    t, d = x2d.shape
    bw = BRANCH_WIDTH
    tm = min(256, seq)
    per_b = seq // tm
    tok = lambda w: pl.BlockSpec((tm, w), lambda i: (i, 0))
    gate = lambda n: pl.BlockSpec((tm, d), lambda i, n=n: (i, n))
    const = lambda shape: pl.BlockSpec(shape, lambda i: (0,) * len(shape))
    return pl.pallas_call(
        _merge_kernel,
        grid=(t // tm,),
        in_specs=[tok(d), pl.BlockSpec((1, 1, d), lambda i: (i // per_b, 0, 0)),
                  tok(bw), tok(bw), tok(bw), tok(bw),
                  gate(0), gate(1), gate(2), gate(3),
                  const((N_BRANCH, bw, d)), const((d, d))],
        out_specs=tok(d),
        out_shape=jax.ShapeDtypeStruct((t, d), F32),
        compiler_params=_cparams(("parallel",)),
        name="merge",
    )(x2d, g1, s5o, hg, sb, po, proj, proj, proj, proj, w_branch.astype(BF16), w_out.astype(BF16))


def _top_values(s, n):
    vals = []
    for r in range(n):
        cur = s if r == 0 else jnp.where(s < vals[-1], s, NEG_BIG)
        vals.append(jnp.max(cur, axis=0, keepdims=True))
    return vals


def _prefix_length(rows, pred):
    c1 = pred(rows[7])
    c2 = pred(jnp.where(c1, rows[11], rows[3]))
    c3 = pred(jnp.where(c1, jnp.where(c2, rows[13], rows[9]), jnp.where(c2, rows[5], rows[1])))
    pick = lambda a: jnp.where(c3, rows[a + 2], rows[a])
    c4 = pred(jnp.where(c1, jnp.where(c2, pick(12), pick(8)), jnp.where(c2, pick(4), pick(0))))
    c5 = pred(rows[15])
    one = lambda c, v: jnp.where(c, v, 0.0)
    return (one(c1, 8.0) + one(c2, 4.0)) + (one(c3, 2.0) + one(c4, 1.0)) + one(c5, 1.0)


def _route_kernel(x_ref, gain_ref, sh_ref, sc_ref, wq_ref, keys_ref,
                  ht_ref, cnt_ref, e1_ref, rank_ref, e2_ref):
    k = PEER_TOPK
    h = _rms_modulate(x_ref[...], gain_ref[...], sh_ref[0], sc_ref[0])
    ht = h.T.astype(BF16)
    ht_ref[...] = ht
    qt = jnp.dot(wq_ref[...], ht, preferred_element_type=F32)
    for hd in range(PEER_HEADS):
        sc = []
        for half in range(2):
            lo = (2 * hd + half) * PEER_NKEYS
            qs = qt[lo:lo + PEER_NKEYS, :].astype(BF16)
            sc.append(jnp.dot(keys_ref[hd, half], qs, preferred_element_type=F32))
        top1 = _top_values(sc[0], k)
        top2 = _top_values(sc[1], k)
        v1 = jnp.concatenate(top1, axis=0)
        v2 = jnp.concatenate(top2, axis=0)
        cand = [top1[0] + v2] + [top1[a] + v2[:8] for a in range(1, 8)] + [v1[8:] + top2[0]]
        best = _top_values(jnp.concatenate(cand, axis=0), k)
        tau = best[k - 1]
        z = None
        for r in range(k):
            e = jnp.exp(best[r] - best[0])
            z = e if z is None else z + e
        s1, s2 = sc
        cnt_ref[hd] = _prefix_length(top2, lambda row: s1 + row >= tau)
        rank_ref[hd] = _prefix_length(top2, lambda row: row > s2).astype(BF16)
        e1_ref[hd] = jnp.exp(s1 - top1[0]) / z
        e2_ref[hd] = jnp.exp(s2 - top2[0]).astype(BF16)


def _route(x2d, gain, shift, scale, wq_t, keys, seq):
    t, d = x2d.shape
    tm = min(256, seq)
    per_b = seq // tm
    nh, nk = PEER_HEADS, PEER_NKEYS
    big = pl.BlockSpec((nh, nk, tm), lambda i: (0, 0, i))
    big_f32 = jax.ShapeDtypeStruct((nh, nk, t), F32)
    big_bf16 = jax.ShapeDtypeStruct((nh, nk, t), BF16)
    return pl.pallas_call(
        _route_kernel,
        grid=(t // tm,),
        in_specs=[pl.BlockSpec((tm, d), lambda i: (i, 0)),
                  pl.BlockSpec((1, d), lambda i: (0, 0)),
                  pl.BlockSpec((1, 1, d), lambda i: (i // per_b, 0, 0)),
                  pl.BlockSpec((1, 1, d), lambda i: (i // per_b, 0, 0)),
                  pl.BlockSpec(wq_t.shape, lambda i: (0, 0)),
                  pl.BlockSpec(keys.shape, lambda i: (0, 0, 0, 0))],
        out_specs=[pl.BlockSpec((d, tm), lambda i: (0, i)), big, big, big, big],
        out_shape=[jax.ShapeDtypeStruct((d, t), BF16), big_f32, big_f32, big_bf16, big_bf16],
        compiler_params=_cparams(("parallel",)),
        name="peer_route",
    )(x2d, gain.reshape(1, d), shift, scale, wq_t, keys)


def _rows_bf16(row, n):
    tile = jnp.broadcast_to(row, (16, row.shape[1])).astype(BF16)
    return jnp.concatenate([tile] * (n // 16), axis=0)


def _peer_kernel(ht_ref, u_ref, vt_ref, cnt_ref, e1_ref, rank_ref, e2_ref, x_ref, g2_ref, fg_ref,
                 o_ref, acc_ref, *, te, tm, final_norm):
    e = pl.program_id(1)
    nk = PEER_NKEYS
    blk = PEER_BLOCK
    rows = blk // nk
    dot = functools.partial(jnp.dot, preferred_element_type=F32)

    @pl.when(e == 0)
    def _():
        acc_ref[...] = jnp.zeros_like(acc_ref)

    nblk = te // blk

    def first(c):
        return dot(u_ref[c * blk:(c + 1) * blk, :], ht_ref[...])

    def gated(c, act):
        parts = []
        for ii in range(rows):
            i = c * rows + ii
            gate = jnp.zeros((nk, tm), BF16)
            for hd in range(PEER_HEADS):
                count = _rows_bf16(cnt_ref[hd, i:i + 1, :], nk)
                e1 = _rows_bf16(e1_ref[hd, i:i + 1, :], nk)
                gate = gate + jnp.where(rank_ref[hd] < count, e1 * e2_ref[hd], jnp.zeros((), BF16))
            parts.append(gate * _gelu(act[ii * nk:(ii + 1) * nk, :]).astype(BF16))
        return jnp.concatenate(parts, axis=0)

    ahead = 2
    acts = {c: first(c) for c in range(min(ahead, nblk))}
    for c in range(nblk):
        p = gated(c, acts.pop(c))
        if c + ahead < nblk:
            acts[c + ahead] = first(c + ahead)
        acc_ref[...] += dot(vt_ref[:, c * blk:(c + 1) * blk], p)

    @pl.when(e == pl.num_programs(1) - 1)
    def _():
        y = x_ref[...] + g2_ref[0] * acc_ref[...].T
        if final_norm:
            y = y * lax.rsqrt(jnp.mean(y * y, axis=-1, keepdims=True) + EPS) * fg_ref[...]
        o_ref[...] = y


def _peer(ht, u_bf, v_bf, cnt, e1, rank2, e2, x2d, g2, final_gain, final_norm, seq):
    t, d = x2d.shape
    ne = u_bf.shape[0]
    tm = min(512, seq)
    te = 1024
    per_b = seq // tm
    nh, nk = PEER_HEADS, PEER_NKEYS
    big = pl.BlockSpec((nh, nk, tm), lambda i, g: (0, 0, i))
    rows = pl.BlockSpec((nh, te // nk, tm), lambda i, g: (0, g, i))
    return pl.pallas_call(
        functools.partial(_peer_kernel, te=te, tm=tm, final_norm=final_norm),
        grid=(t // tm, ne // te),
        in_specs=[pl.BlockSpec((d, tm), lambda i, g: (0, i)),
                  pl.BlockSpec((te, d), lambda i, g: (g, 0)),
                  pl.BlockSpec((d, te), lambda i, g: (0, g)),
                  rows, rows, big, big,
                  pl.BlockSpec((tm, d), lambda i, g: (i, 0)),
                  pl.BlockSpec((1, 1, d), lambda i, g: (i // per_b, 0, 0)),
                  pl.BlockSpec((1, d), lambda i, g: (0, 0))],
        out_specs=pl.BlockSpec((tm, d), lambda i, g: (i, 0)),
        out_shape=jax.ShapeDtypeStruct((t, d), F32),
        scratch_shapes=[pltpu.VMEM((d, tm), F32)],
        compiler_params=_cparams(("parallel", "arbitrary"), vmem=PEER_VMEM_LIMIT),
        name="peer_experts",
    )(ht, u_bf, v_bf.T, cnt, e1, rank2, e2, x2d, g2, final_gain.astype(F32).reshape(1, d))


def _layer(x2d, c, bsz, seq, ada_w, ada_b, norm_mix_gain, norm_ffn_gain, w_in, s5_params, s5_d,
           s5_w_glu, s5_b_glu, lower_bound, hg_norm_gain, pool_w, pool_scale, w_gate, w_branch, w_out,
           peer_w_query, peer_sub_keys, peer_u, peer_v, final_gain, last):
    d = D_MODEL
    mod = _ada(c.astype(F32), ada_w.astype(F32), ada_b.astype(F32))
    sh1, sc1, g1, sh2, sc2, g2 = [m.reshape(bsz, 1, d) for m in jnp.split(mod, 6, axis=-1)]
    wcat = jnp.concatenate([w_gate[n] for n in range(N_BRANCH)] + [w_in], axis=1).astype(BF16)
    proj = _inproj(x2d, norm_mix_gain.astype(F32), sh1, sc1, wcat, seq)
    proj3 = proj.reshape(bsz, seq, CAT_COLS)
    bbd, cbd, tabs = _s5_tables(*s5_params)
    s5o = _s5(proj3, bbd, cbd, tabs, s5_d, s5_w_glu, s5_b_glu)
    hg = _hgrn(proj3, lower_bound, hg_norm_gain.astype(F32))
    sb = _stick_breaking(proj3)
    po = _pool(proj3, pool_w, pool_scale)
    t = bsz * seq
    flat = lambda a: a.reshape(t, BRANCH_WIDTH)
    x1 = _merge(x2d, g1, flat(s5o), flat(hg), flat(sb), flat(po), proj, w_branch, w_out, seq)
    ht, cnt, e1, rank2, e2 = _route(x1, norm_ffn_gain.astype(F32), sh2, sc2,
                                    peer_w_query.T.astype(BF16), peer_sub_keys.astype(BF16), seq)
    return _peer(ht, peer_u.astype(BF16), peer_v.astype(BF16), cnt, e1, rank2, e2, x1, g2, final_gain, last,
                 seq)


def kernel(x, c, ada_w, ada_b, norm_mix_gain, norm_ffn_gain, w_in, s5_lambda_re, s5_lambda_im, s5_log_step, s5_b_re, s5_b_im, s5_c_re, s5_c_im, s5_d, s5_w_glu, s5_b_glu, hg_lb_logits, hg_norm_gain, pool_w, pool_scale, w_gate, w_branch, w_out, peer_w_query, peer_sub_keys, peer_u, peer_v, final_gain):
    bsz, seq, d = x.shape
    depth = ada_w.shape[0]
    lb_soft = jax.nn.softmax(hg_lb_logits.astype(F32), axis=0)
    lower_bounds = jnp.cumsum(lb_soft, axis=0) - lb_soft[0:1]
    x2d = x.astype(F32).reshape(bsz * seq, d)
    for l in range(depth):
        s5_params = (s5_lambda_re[l], s5_lambda_im[l], s5_log_step[l], s5_b_re[l], s5_b_im[l],
                     s5_c_re[l], s5_c_im[l])
        x2d = _layer(x2d, c, bsz, seq, ada_w[l], ada_b[l], norm_mix_gain[l], norm_ffn_gain[l], w_in[l],
                     s5_params, s5_d[l], s5_w_glu[l], s5_b_glu[l], lower_bounds[l], hg_norm_gain[l],
                     pool_w[l], pool_scale[l], w_gate[l], w_branch[l], w_out[l], peer_w_query[l],
                     peer_sub_keys[l], peer_u[l], peer_v[l], final_gain, l == depth - 1)
    return x2d.reshape(bsz, seq, d).astype(x.dtype)
```

```python
import functools
import math

import jax
import jax.numpy as jnp
from jax import lax
from jax.experimental import pallas as pl
from jax.experimental.pallas import tpu as pltpu

F32 = jnp.float32
BF16 = jnp.bfloat16

D_MODEL = 2048
BRANCH_WIDTH = 512
N_BRANCH = 4
LANE = 128
S5_GROUP = 16
S5_GROUPS = 32
S5_STATE = 64
S5_MIN_NEG = 1e-4
S5_ROWS = 8
HG_HEADS = 4
HG_F_MIN = 1e-6
HG_K_MIN = 1e-30
HG_CHUNK = 128
HG_SUB = 16
HG_STEP_HEADS = 4
SB_HEADS = 4
SB_DIM = 128
SB_TILE = 256
POOL_WINDOWS = (2, 4, 8, 16)
PEER_HEADS = 8
PEER_NKEYS = 128
PEER_TOPK = 16
PEER_BLOCK = 256
EPS = 1e-6
NEG_BIG = -3.0e38

GATE_COLS = N_BRANCH * D_MODEL
IN_COLS = 9 * BRANCH_WIDTH
CAT_COLS = GATE_COLS + IN_COLS
PROJ_BLK0 = GATE_COLS // LANE
INPROJ_CHUNK = 256
VMEM_LIMIT = 48 * 1024 * 1024
PEER_VMEM_LIMIT = 56 * 1024 * 1024


def _cparams(semantics, vmem=VMEM_LIMIT):
    return pltpu.CompilerParams(dimension_semantics=semantics, vmem_limit_bytes=vmem)


def _split3(x):
    hi = x.astype(BF16)
    r1 = x - hi.astype(F32)
    mid = r1.astype(BF16)
    lo = (r1 - mid.astype(F32)).astype(BF16)
    return hi, mid, lo


def _dot01_left(m01, x):
    hi, mid, lo = _split3(x)
    d = functools.partial(jnp.dot, preferred_element_type=F32)
    return d(m01, hi) + d(m01, mid) + d(m01, lo)


def _dot01_right2(x, m01):
    hi = x.astype(BF16)
    lo = (x - hi.astype(F32)).astype(BF16)
    d = functools.partial(jnp.dot, preferred_element_type=F32)
    return d(hi, m01) + d(lo, m01)


def _dot_f32(a, b):
    ah, am, al = _split3(a)
    bh, bm, bl = _split3(b)
    d = functools.partial(jnp.dot, preferred_element_type=F32)
    return (d(ah, bh) + (d(ah, bm) + d(am, bh))) + ((d(ah, bl) + d(al, bh)) + d(am, bm))


def _gelu(x):
    return 0.5 * x * (1.0 + lax.erf(x * (1.0 / math.sqrt(2.0))))


def _sigmoid(x):
    return 0.5 * jnp.tanh(0.5 * x) + 0.5


def _silu(x):
    return x * _sigmoid(x)


def _rms_modulate(x, gain, shift, scale):
    y = x * lax.rsqrt(jnp.mean(x * x, axis=-1, keepdims=True) + EPS) * gain
    return y * (1.0 + scale) + shift


def _ada_kernel(c_ref, w_ref, b_ref, o_ref):
    o_ref[...] = _dot_f32(_silu(c_ref[...]), w_ref[...]) + b_ref[...]


def _ada(c, w, b):
    bsz, d = c.shape
    n = w.shape[1]
    tn = 1024
    return pl.pallas_call(
        _ada_kernel,
        grid=(n // tn,),
        in_specs=[pl.BlockSpec((bsz, d), lambda j: (0, 0)),
                  pl.BlockSpec((d, tn), lambda j: (0, j)),
                  pl.BlockSpec((1, tn), lambda j: (0, j))],
        out_specs=pl.BlockSpec((bsz, tn), lambda j: (0, j)),
        out_shape=jax.ShapeDtypeStruct((bsz, n), F32),
        compiler_params=_cparams(("arbitrary",)),
        name="ada_mod",
    )(c, w, b.reshape(1, n))


def _inproj_kernel(x_ref, gain_ref, sh_ref, sc_ref, w_ref, o_ref, h_ref, *, gate_chunks):
    j = pl.program_id(1)

    @pl.when(j == 0)
    def _():
        def rows(r, c):
            sl = pl.ds(pl.multiple_of(r * LANE, LANE), LANE)
            h_ref[sl, :] = _rms_modulate(x_ref[sl, :], gain_ref[...], sh_ref[0], sc_ref[0]).astype(BF16)
            return c

        lax.fori_loop(0, x_ref.shape[0] // LANE, rows, 0)

    cw = INPROJ_CHUNK
    nch = w_ref.shape[1] // cw
    mm = lambda k: jnp.dot(h_ref[...], w_ref[:, k * cw:(k + 1) * cw], preferred_element_type=F32)
    parts = {0: mm(0), 1: mm(1)}
    for k in range(nch):
        r = parts.pop(k)
        if k + 2 < nch:
            parts[k + 2] = mm(k + 2)
        is_gate = j * nch + k < gate_chunks
        o_ref[:, k * cw:(k + 1) * cw] = jnp.where(is_gate, _sigmoid(r), r).astype(o_ref.dtype)


def _inproj(x2d, gain, shift, scale, wcat, seq):
    t, d = x2d.shape
    n = wcat.shape[1]
    tm = min(1024, seq)
    tn = 10 * INPROJ_CHUNK
    per_b = seq // tm
    return pl.pallas_call(
        functools.partial(_inproj_kernel, gate_chunks=GATE_COLS // INPROJ_CHUNK),
        grid=(t // tm, n // tn),
        in_specs=[pl.BlockSpec((tm, d), lambda i, j: (i, 0)),
                  pl.BlockSpec((1, d), lambda i, j: (0, 0)),
                  pl.BlockSpec((1, 1, d), lambda i, j: (i // per_b, 0, 0)),
                  pl.BlockSpec((1, 1, d), lambda i, j: (i // per_b, 0, 0)),
                  pl.BlockSpec((d, tn), lambda i, j: (0, j))],
        out_specs=pl.BlockSpec((tm, tn), lambda i, j: (i, j)),
        out_shape=jax.ShapeDtypeStruct((t, n), BF16),
        scratch_shapes=[pltpu.VMEM((tm, d), BF16)],
        compiler_params=_cparams(("parallel", "arbitrary"), vmem=PEER_VMEM_LIMIT),
        name="inproj",
    )(x2d, gain.reshape(1, d), shift, scale, wcat)


def _s5_tables(lam_re, lam_im, log_step, b_re, b_im, c_re, c_im):
    g, p, h = S5_GROUPS, S5_STATE, S5_GROUP
    lr = jnp.minimum(lam_re.astype(F32), -S5_MIN_NEG)
    li = lam_im.astype(F32)
    step = jnp.exp(log_step.astype(F32))[:, None]
    mag = jnp.exp(lr * step)
    ab_re = mag * jnp.cos(li * step)
    ab_im = mag * jnp.sin(li * step)
    den = lr * lr + li * li
    nr = ab_re - 1.0
    fr = (nr * lr + ab_im * li) / den
    fi = (ab_im * lr - nr * li) / den
    br, bi = b_re.astype(F32), b_im.astype(F32)
    bb_re = fr[..., None] * br - fi[..., None] * bi
    bb_im = fr[..., None] * bi + fi[..., None] * br
    eye_g = jnp.eye(g, dtype=F32)

    def in_map(m):
        return jnp.einsum('gph,gk->ghkp', m, eye_g).reshape(g * h, g * p)

    def out_map(m):
        return jnp.einsum('ghp,gk->gpkh', m, eye_g).reshape(g * p, g * h)

    bbd = jnp.concatenate([in_map(bb_re), in_map(bb_im)], axis=1).astype(BF16)
    cbd = jnp.concatenate([out_map(c_re.astype(F32)), out_map(-c_im.astype(F32))], axis=0).astype(BF16)
    prs, pis = [ab_re], [ab_im]
    for _ in range(S5_ROWS - 1):
        pr_, pi_ = prs[-1], pis[-1]
        prs.append(pr_ * ab_re - pi_ * ab_im)
        pis.append(pr_ * ab_im + pi_ * ab_re)
    rows = jnp.arange(S5_ROWS)[:, None]
    tabs = []
    for d in (1, 2, 4):
        tabs.append(jnp.where(rows >= d, prs[d - 1].reshape(1, g * p), 0.0))
        tabs.append(jnp.where(rows >= d, pis[d - 1].reshape(1, g * p), 0.0))
    tabs.append(jnp.stack(prs).reshape(S5_ROWS, g * p))
    tabs.append(jnp.stack(pis).reshape(S5_ROWS, g * p))
    return bbd, cbd, jnp.stack(tabs).astype(F32)


def _s5_kernel(u_ref, bbd_ref, tab_ref, cbd_ref, d_ref, wglu_ref, bglu_ref, o_ref,
               bu_scr, x_scr, carry_scr, *, tile):
    ns = S5_GROUPS * S5_STATE
    lc = 512

    @pl.when(pl.program_id(1) == 0)
    def _():
        carry_scr[...] = jnp.zeros_like(carry_scr)

    u = u_ref[0]
    dot = functools.partial(jnp.dot, preferred_element_type=F32)
    gl = lc // (S5_STATE // S5_GROUP)
    nch = ns // lc

    def project(ch):
        ub = u[:, ch * gl:(ch + 1) * gl]
        for part in (0, ns):
            cols = slice(part + ch * lc, part + (ch + 1) * lc)
            bu_scr[:, cols] = dot(ub, bbd_ref[ch * gl:(ch + 1) * gl, cols])

    def scan(ch):
        re_l = slice(ch * lc, (ch + 1) * lc)
        im_l = slice(ns + ch * lc, ns + (ch + 1) * lc)
        cr = carry_scr[:, re_l]
        ci = carry_scr[:, im_l]
        for r in range(tile // S5_ROWS):
            rows = slice(r * S5_ROWS, (r + 1) * S5_ROWS)
            xr = bu_scr[rows, re_l]
            xi = bu_scr[rows, im_l]
            for k, d in enumerate((1, 2, 4)):
                sr = pltpu.roll(xr, d, 0)
                si = pltpu.roll(xi, d, 0)
                mr = tab_ref[2 * k, :, re_l]
                mi = tab_ref[2 * k + 1, :, re_l]
                xr, xi = xr + (mr * sr - mi * si), xi + (mr * si + mi * sr)
            ar = tab_ref[6, :, re_l]
            ai = tab_ref[7, :, re_l]
            xr, xi = xr + (ar * cr - ai * ci), xi + (ar * ci + ai * cr)
            x_scr[rows, re_l] = xr
            x_scr[rows, im_l] = xi
            cr = jnp.broadcast_to(xr[S5_ROWS - 1:S5_ROWS], (S5_ROWS, lc))
            ci = jnp.broadcast_to(xi[S5_ROWS - 1:S5_ROWS], (S5_ROWS, lc))
        carry_scr[:, re_l] = cr
        carry_scr[:, im_l] = ci

    def readout(ch):
        out_l = slice(ch * gl, (ch + 1) * gl)
        return sum(dot(x_scr[:, part + ch * lc:part + (ch + 1) * lc].astype(BF16),
                       cbd_ref[part + ch * lc:part + (ch + 1) * lc, out_l]) for part in (0, ns))

    project(0)
    ys = []
    for ch in range(nch):
        if ch + 1 < nch:
            project(ch + 1)
        scan(ch)
        ys.append(readout(ch))
    y = _gelu(jnp.concatenate(ys, axis=1) + d_ref[...] * u.astype(F32))
    gate = _sigmoid(dot(y.astype(BF16), wglu_ref[...]) + bglu_ref[...])
    o_ref[0] = (y * gate).astype(o_ref.dtype)


def _s5(proj3, bbd, cbd, tabs, d_skip, w_glu, b_glu):
    bsz, seq, _ = proj3.shape
    bw = BRANCH_WIDTH
    ns = S5_GROUPS * S5_STATE
    tile = min(256, seq)
    const = lambda shape: pl.BlockSpec(shape, lambda b, i: (0,) * len(shape))
    return pl.pallas_call(
        functools.partial(_s5_kernel, tile=tile),
        grid=(bsz, seq // tile),
        in_specs=[pl.BlockSpec((1, tile, bw), lambda b, i: (b, i, GATE_COLS // bw)),
                  const((bw, 2 * ns)), const((8, S5_ROWS, ns)), const((2 * ns, bw)),
                  const((1, bw)), const((bw, bw)), const((1, bw))],
        out_specs=pl.BlockSpec((1, tile, bw), lambda b, i: (b, i, 0)),
        out_shape=jax.ShapeDtypeStruct((bsz, seq, bw), BF16),
        scratch_shapes=[pltpu.VMEM((tile, 2 * ns), F32), pltpu.VMEM((tile, 2 * ns), F32),
                        pltpu.VMEM((S5_ROWS, 2 * ns), F32)],
        compiler_params=_cparams(("parallel", "arbitrary")),
        name="s5",
    )(proj3, bbd, tabs, cbd, d_skip.astype(F32).reshape(1, bw), w_glu.astype(BF16),
      b_glu.astype(F32).reshape(1, bw))


def _hgrn_kernel(q_ref, f_ref, i_ref, g_ref, lb_ref, gain_ref, tril_ref, o_ref, st_scr, *, seq):
    c, sub = HG_CHUNK, HG_SUB
    tril = tril_ref[...]
    row = lax.broadcasted_iota(jnp.int32, (c, LANE), 0)
    rsub = jnp.bitwise_and(row, sub - 1)
    tcol = lax.broadcasted_iota(jnp.int32, (sub, c), 1)
    st_scr[...] = jnp.zeros_like(st_scr)
    nt = (((1,), (1,)), ((), ()))

    def head_chunk(off, hh):
        hl = slice(hh * LANE, (hh + 1) * LANE)
        lb = lb_ref[0, :, hl]
        gain = gain_ref[0, :, hl]
        q = _silu(q_ref[0, pl.ds(off, c), hl].astype(F32))
        fg = lb + (1.0 - lb) * _sigmoid(f_ref[0, pl.ds(off, c), hl].astype(F32))
        logf = jnp.log(jnp.maximum(fg, HG_F_MIN))
        k = 1.0 - fg
        v = i_ref[0, pl.ds(off, c), hl]
        vf = v.astype(F32)
        bc = _dot01_left(tril, logf)
        bk = bc - jnp.log(jnp.maximum(k, HG_K_MIN))
        o = jnp.zeros((c, LANE), F32)
        for lag in range(sub):
            if lag == 0:
                bd, vd = bk, vf
            else:
                bd = pltpu.roll(bk, lag, 0)
                vd = pltpu.roll(vf, lag, 0)
            valid = rsub >= lag
            e = jnp.exp(jnp.where(valid, bc - bd, 0.0))
            w = jnp.sum(q * e, axis=-1, keepdims=True)
            o = o + jnp.where(valid, w * vd, 0.0)
        parts = []
        for j in range(c // sub - 1):
            lo, hi = sub * j, sub * (j + 1)
            bend = bc[hi - 1:hi, :]
            qj = (q * jnp.exp(jnp.minimum(bc - bend, 0.0))).astype(BF16)
            kj = (k[lo:hi] * jnp.exp(bend - bc[lo:hi])).astype(BF16)
            st = lax.dot_general(kj, qj, nt, preferred_element_type=F32)
            parts.append(jnp.where(tcol >= hi, st, 0.0))
        parts.append(jnp.zeros((sub, c), F32))
        scores = jnp.concatenate(parts, axis=0).T
        o = o + jnp.dot(scores.astype(BF16), v, preferred_element_type=F32)
        st_prev = st_scr[hh]
        qe = (q * jnp.exp(bc)).astype(BF16)
        o = o + lax.dot_general(qe, st_prev.astype(BF16), nt, preferred_element_type=F32)
        blast = bc[c - 1:c, :]
        kdec = (k * jnp.exp(blast - bc)).astype(BF16)
        st_scr[hh] = st_prev * jnp.exp(blast) + jnp.dot(vf.T.astype(BF16), kdec,
                                                        preferred_element_type=F32)
        o = o * lax.rsqrt(jnp.mean(o * o, axis=-1, keepdims=True) + EPS) * gain
        o = o * _silu(g_ref[0, pl.ds(off, c), hl].astype(F32))
        o_ref[0, pl.ds(off, c), hl] = o.astype(o_ref.dtype)

    def chunk(ci, carry):
        off = pl.multiple_of(ci * c, c)
        for hh in range(HG_STEP_HEADS):
            head_chunk(off, hh)
        return carry

    lax.fori_loop(0, seq // c, chunk, 0)


def _hgrn(proj3, lower_bound, norm_gain):
    bsz, seq, _ = proj3.shape
    c = HG_CHUNK
    tril = jnp.tril(jnp.ones((c, c), F32)).astype(BF16)
    base = PROJ_BLK0 + BRANCH_WIDTH // LANE

    pw = HG_STEP_HEADS * LANE

    def col(k):
        return pl.BlockSpec((1, seq, pw), lambda b, h: (b, 0, (base + k * HG_HEADS) // HG_STEP_HEADS + h))

    vec = pl.BlockSpec((1, 1, pw), lambda b, h: (h, 0, 0))
    return pl.pallas_call(
        functools.partial(_hgrn_kernel, seq=seq),
        grid=(bsz, HG_HEADS // HG_STEP_HEADS),
        in_specs=[col(0), col(1), col(2), col(3), vec, vec,
                  pl.BlockSpec((c, c), lambda b, h: (0, 0))],
        out_specs=pl.BlockSpec((1, seq, pw), lambda b, h: (b, 0, h)),
        out_shape=jax.ShapeDtypeStruct((bsz, seq, BRANCH_WIDTH), BF16),
        scratch_shapes=[pltpu.VMEM((HG_STEP_HEADS, LANE, LANE), F32)],
        compiler_params=_cparams(("parallel", "parallel")),
        name="hgrn2",
    )(proj3, proj3, proj3, proj3, lower_bound.reshape(HG_HEADS // HG_STEP_HEADS, 1, pw),
      norm_gain.reshape(HG_HEADS // HG_STEP_HEADS, 1, pw), tril)


def _sb_kernel(q_ref, k_ref, v_ref, up_ref, o_ref, acc_scr, later_scr):
    qi = pl.program_id(1)
    t = SB_TILE
    up = up_ref[...]
    scale = 1.0 / math.sqrt(SB_DIM)
    row = lax.broadcasted_iota(jnp.int32, (t, t), 0)
    colv = lax.broadcasted_iota(jnp.int32, (t, t), 1)
    below = colv < row
    nt = (((1,), (1,)), ((), ()))
    acc_scr[...] = jnp.zeros_like(acc_scr)
    later_scr[...] = jnp.zeros_like(later_scr)

    def tile(jj, diagonal):
        off = pl.multiple_of((qi - jj) * t, t)
        keep = (lambda a: jnp.where(below, a, 0.0)) if diagonal else (lambda a: a)
        heads = [slice(h * SB_DIM, (h + 1) * SB_DIM) for h in range(SB_HEADS)]
        zs = [lax.dot_general(q_ref[0, :, hl], k_ref[0, pl.ds(off, t), hl], nt,
                              preferred_element_type=F32) * scale for hl in heads]
        logits, suffixes, totals = [], [], []
        for h in range(SB_HEADS):
            z = zs[h]
            sp = jnp.maximum(z, 0.0) + jnp.log(1.0 + jnp.exp(-jnp.abs(z)))
            log_not = keep(-sp)
            suffixes.append(_dot01_right2(log_not, up))
            logits.append(z - sp)
            totals.append(jnp.sum(log_not, axis=-1, keepdims=True))
        for h in range(SB_HEADS):
            later = later_scr[h]
            w = keep(jnp.exp(logits[h] + suffixes[h] + later))
            acc_scr[h] += jnp.dot(w.astype(BF16), v_ref[0, pl.ds(off, t), heads[h]],
                                  preferred_element_type=F32)
            later_scr[h] = later + totals[h]

    def body(jj, carry):
        tile(jj, diagonal=False)
        return carry

    tile(0, diagonal=True)
    lax.fori_loop(1, qi + 1, body, 0)
    for h in range(SB_HEADS):
        o_ref[0, :, h * SB_DIM:(h + 1) * SB_DIM] = acc_scr[h].astype(o_ref.dtype)


def _stick_breaking(proj3):
    bsz, seq, _ = proj3.shape
    t = SB_TILE
    bw = BRANCH_WIDTH
    base = (GATE_COLS + 5 * bw) // bw
    up = (jnp.arange(t)[:, None] > jnp.arange(t)[None, :]).astype(BF16)
    return pl.pallas_call(
        _sb_kernel,
        grid=(bsz, seq // t),
        in_specs=[pl.BlockSpec((1, t, bw), lambda b, i: (b, i, base)),
                  pl.BlockSpec((1, seq, bw), lambda b, i: (b, 0, base + 1)),
                  pl.BlockSpec((1, seq, bw), lambda b, i: (b, 0, base + 2)),
                  pl.BlockSpec((t, t), lambda b, i: (0, 0))],
        out_specs=pl.BlockSpec((1, t, bw), lambda b, i: (b, i, 0)),
        out_shape=jax.ShapeDtypeStruct((bsz, seq, bw), BF16),
        scratch_shapes=[pltpu.VMEM((SB_HEADS, t, SB_DIM), F32), pltpu.VMEM((SB_HEADS, t, 1), F32)],
        compiler_params=_cparams(("parallel", "arbitrary")),
        name="stick_breaking",
    )(proj3, proj3, proj3, up)


def _pool_kernel(p_ref, bc_ref, bp_ref, w_ref, sc_ref, o_ref, *, seq):
    g = pl.program_id(1)
    tile = LANE
    window = lax.shift_left(jnp.int32(2), g)
    row = lax.broadcasted_iota(jnp.int32, (tile, LANE), 0)
    band_cur = bc_ref[0]
    band_prev = bp_ref[0]
    wmix = w_ref[0]
    scale = sc_ref[0]

    group = min(4, seq // tile)

    def body(it, carry):
        offs, curs, wins = [], [], []
        for k in range(group):
            i = it * group + k
            off = pl.multiple_of(i * tile, tile)
            poff = pl.multiple_of(jnp.maximum(i - 1, 0) * tile, tile)
            cur = p_ref[0, pl.ds(off, tile), :]
            prev = p_ref[0, pl.ds(poff, tile), :]
            win = jnp.dot(band_cur, cur, preferred_element_type=F32)
            win = win + jnp.where(i > 0, jnp.dot(band_prev, prev, preferred_element_type=F32), 0.0)
            offs.append(off)
            curs.append(cur)
            wins.append(win)
        for k in range(group):
            count = jnp.minimum(row + (offs[k] + 1), window).astype(F32)
            pooled = wins[k] / count - curs[k].astype(F32)
            mixed = jnp.dot(pooled.astype(BF16), wmix, preferred_element_type=F32) * scale
            o_ref[0, pl.ds(offs[k], tile), :] = mixed.astype(o_ref.dtype)
        return carry

    lax.fori_loop(0, seq // (tile * group), body, 0)


def _pool(proj3, pool_w, pool_scale):
    bsz, seq, _ = proj3.shape
    ng = len(POOL_WINDOWS)
    base = PROJ_BLK0 + 8 * BRANCH_WIDTH // LANE
    t = jnp.arange(LANE)
    lag = t[:, None] - t[None, :]
    wins = jnp.asarray(POOL_WINDOWS)[:, None, None]
    band_cur = ((lag[None] >= 0) & (lag[None] < wins)).astype(BF16)
    band_prev = (((lag[None] + LANE) >= 0) & ((lag[None] + LANE) < wins)).astype(BF16)
    mat = pl.BlockSpec((1, LANE, LANE), lambda b, g: (g, 0, 0))
    return pl.pallas_call(
        functools.partial(_pool_kernel, seq=seq),
        grid=(bsz, ng),
        in_specs=[pl.BlockSpec((1, seq, LANE), lambda b, g: (b, 0, base + g)),
                  mat, mat, mat,
                  pl.BlockSpec((1, 1, LANE), lambda b, g: (g, 0, 0))],
        out_specs=pl.BlockSpec((1, seq, LANE), lambda b, g: (b, 0, g)),
        out_shape=jax.ShapeDtypeStruct((bsz, seq, BRANCH_WIDTH), BF16),
        compiler_params=_cparams(("parallel", "parallel")),
        name="pool",
    )(proj3, band_cur, band_prev, pool_w.astype(BF16), pool_scale.astype(F32).reshape(ng, 1, LANE))


def _merge_kernel(x_ref, g1_ref, s5_ref, hg_ref, sb_ref, po_ref, gt0_ref, gt1_ref, gt2_ref, gt3_ref,
                  wbr_ref, wout_ref, o_ref):
    dot = functools.partial(jnp.dot, preferred_element_type=F32)
    branches = (s5_ref, hg_ref, sb_ref, po_ref)
    gates = (gt0_ref, gt1_ref, gt2_ref, gt3_ref)
    merged = None
    for n in range(N_BRANCH):
        term = gates[n][...].astype(F32) * dot(branches[n][...], wbr_ref[n])
        merged = term if merged is None else merged + term
    mix = dot(merged.astype(BF16), wout_ref[...])
    o_ref[...] = x_ref[...] + g1_ref[0] * mix


def _merge(x2d, g1, s5o, hg, sb, po, proj, w_branch, w_out, seq):
    t, d = x2d.shape
    bw = BRANCH_WIDTH
    tm = min(256, seq)
    per_b = seq // tm
    tok = lambda w: pl.BlockSpec((tm, w), lambda i: (i, 0))
    gate = lambda n: pl.BlockSpec((tm, d), lambda i, n=n: (i, n))
    const = lambda shape: pl.BlockSpec(shape, lambda i: (0,) * len(shape))
    return pl.pallas_call(
        _merge_kernel,
        grid=(t // tm,),
        in_specs=[tok(d), pl.BlockSpec((1, 1, d), lambda i: (i // per_b, 0, 0)),
                  tok(bw), tok(bw), tok(bw), tok(bw),
                  gate(0), gate(1), gate(2), gate(3),
                  const((N_BRANCH, bw, d)), const((d, d))],
        out_specs=tok(d),
        out_shape=jax.ShapeDtypeStruct((t, d), F32),
        compiler_params=_cparams(("parallel",)),
        name="merge",
    )(x2d, g1, s5o, hg, sb, po, proj, proj, proj, proj, w_branch.astype(BF16), w_out.astype(BF16))


def _top_values(s, n):
    vals = []
    for r in range(n):
        cur = s if r == 0 else jnp.where(s < vals[-1], s, NEG_BIG)
        vals.append(jnp.max(cur, axis=0, keepdims=True))
    return vals


def _prefix_length(rows, pred):
    c1 = pred(rows[7])
    c2 = pred(jnp.where(c1, rows[11], rows[3]))
    c3 = pred(jnp.where(c1, jnp.where(c2, rows[13], rows[9]), jnp.where(c2, rows[5], rows[1])))
    pick = lambda a: jnp.where(c3, rows[a + 2], rows[a])
    c4 = pred(jnp.where(c1, jnp.where(c2, pick(12), pick(8)), jnp.where(c2, pick(4), pick(0))))
    c5 = pred(rows[15])
    one = lambda c, v: jnp.where(c, v, 0.0)
    return (one(c1, 8.0) + one(c2, 4.0)) + (one(c3, 2.0) + one(c4, 1.0)) + one(c5, 1.0)


def _route_kernel(x_ref, gain_ref, sh_ref, sc_ref, wq_ref, keys_ref,
                  ht_ref, cnt_ref, e1_ref, rank_ref, e2_ref):
    k = PEER_TOPK
    h = _rms_modulate(x_ref[...], gain_ref[...], sh_ref[0], sc_ref[0])
    ht = h.T.astype(BF16)
    ht_ref[...] = ht
    qt = jnp.dot(wq_ref[...], ht, preferred_element_type=F32)
    for hd in range(PEER_HEADS):
        sc = []
        for half in range(2):
            lo = (2 * hd + half) * PEER_NKEYS
            qs = qt[lo:lo + PEER_NKEYS, :].astype(BF16)
            sc.append(jnp.dot(keys_ref[hd, half], qs, preferred_element_type=F32))
        top1 = _top_values(sc[0], k)
        top2 = _top_values(sc[1], k)
        v1 = jnp.concatenate(top1, axis=0)
        v2 = jnp.concatenate(top2, axis=0)
        cand = [top1[0] + v2] + [top1[a] + v2[:8] for a in range(1, 8)] + [v1[8:] + top2[0]]
        best = _top_values(jnp.concatenate(cand, axis=0), k)
        tau = best[k - 1]
        z = None
        for r in range(k):
            e = jnp.exp(best[r] - best[0])
            z = e if z is None else z + e
        s1, s2 = sc
        cnt_ref[hd] = _prefix_length(top2, lambda row: s1 + row >= tau)
        rank_ref[hd] = _prefix_length(top2, lambda row: row > s2).astype(BF16)
        e1_ref[hd] = jnp.exp(s1 - top1[0]) / z
        e2_ref[hd] = jnp.exp(s2 - top2[0]).astype(BF16)


def _route(x2d, gain, shift, scale, wq_t, keys, seq):
    t, d = x2d.shape
    tm = min(256, seq)
    per_b = seq // tm
    nh, nk = PEER_HEADS, PEER_NKEYS
    big = pl.BlockSpec((nh, nk, tm), lambda i: (0, 0, i))
    big_f32 = jax.ShapeDtypeStruct((nh, nk, t), F32)
    big_bf16 = jax.ShapeDtypeStruct((nh, nk, t), BF16)
    return pl.pallas_call(
        _route_kernel,
        grid=(t // tm,),
        in_specs=[pl.BlockSpec((tm, d), lambda i: (i, 0)),
                  pl.BlockSpec((1, d), lambda i: (0, 0)),
                  pl.BlockSpec((1, 1, d), lambda i: (i // per_b, 0, 0)),
                  pl.BlockSpec((1, 1, d), lambda i: (i // per_b, 0, 0)),
                  pl.BlockSpec(wq_t.shape, lambda i: (0, 0)),
                  pl.BlockSpec(keys.shape, lambda i: (0, 0, 0, 0))],
        out_specs=[pl.BlockSpec((d, tm), lambda i: (0, i)), big, big, big, big],
        out_shape=[jax.ShapeDtypeStruct((d, t), BF16), big_f32, big_f32, big_bf16, big_bf16],
        compiler_params=_cparams(("parallel",)),
        name="peer_route",
    )(x2d, gain.reshape(1, d), shift, scale, wq_t, keys)


def _rows_bf16(row, n):
    tile = jnp.broadcast_to(row, (16, row.shape[1])).astype(BF16)
    return jnp.concatenate([tile] * (n // 16), axis=0)


def _peer_kernel(ht_ref, u_ref, vt_ref, cnt_ref, e1_ref, rank_ref, e2_ref, x_ref, g2_ref, fg_ref,
                 o_ref, acc_ref, *, te, tm, final_norm):
    e = pl.program_id(1)
    nk = PEER_NKEYS
    blk = PEER_BLOCK
    rows = blk // nk
    dot = functools.partial(jnp.dot, preferred_element_type=F32)

    @pl.when(e == 0)
    def _():
        acc_ref[...] = jnp.zeros_like(acc_ref)

    nblk = te // blk

    def first(c):
        return dot(u_ref[c * blk:(c + 1) * blk, :], ht_ref[...])

    def gated(c, act):
        parts = []
        for ii in range(rows):
            i = c * rows + ii
            gate = jnp.zeros((nk, tm), BF16)
            for hd in range(PEER_HEADS):
                count = _rows_bf16(cnt_ref[hd, i:i + 1, :], nk)
                e1 = _rows_bf16(e1_ref[hd, i:i + 1, :], nk)
                gate = gate + jnp.where(rank_ref[hd] < count, e1 * e2_ref[hd], jnp.zeros((), BF16))
            parts.append(gate * _gelu(act[ii * nk:(ii + 1) * nk, :]).astype(BF16))
        return jnp.concatenate(parts, axis=0)

    ahead = 2
    acts = {c: first(c) for c in range(min(ahead, nblk))}
    for c in range(nblk):
        p = gated(c, acts.pop(c))
        if c + ahead < nblk:
            acts[c + ahead] = first(c + ahead)
        acc_ref[...] += dot(vt_ref[:, c * blk:(c + 1) * blk], p)

    @pl.when(e == pl.num_programs(1) - 1)
    def _():
        y = x_ref[...] + g2_ref[0] * acc_ref[...].T
        if final_norm:
            y = y * lax.rsqrt(jnp.mean(y * y, axis=-1, keepdims=True) + EPS) * fg_ref[...]
        o_ref[...] = y


def _peer(ht, u_bf, v_bf, cnt, e1, rank2, e2, x2d, g2, final_gain, final_norm, seq):
    t, d = x2d.shape
    ne = u_bf.shape[0]
    tm = min(512, seq)
    te = 1024
    per_b = seq // tm
    nh, nk = PEER_HEADS, PEER_NKEYS
    big = pl.BlockSpec((nh, nk, tm), lambda i, g: (0, 0, i))
    rows = pl.BlockSpec((nh, te // nk, tm), lambda i, g: (0, g, i))
    return pl.pallas_call(
        functools.partial(_peer_kernel, te=te, tm=tm, final_norm=final_norm),
        grid=(t // tm, ne // te),
        in_specs=[pl.BlockSpec((d, tm), lambda i, g: (0, i)),
                  pl.BlockSpec((te, d), lambda i, g: (g, 0)),
                  pl.BlockSpec((d, te), lambda i, g: (0, g)),
                  rows, rows, big, big,
                  pl.BlockSpec((tm, d), lambda i, g: (i, 0)),
                  pl.BlockSpec((1, 1, d), lambda i, g: (i // per_b, 0, 0)),
                  pl.BlockSpec((1, d), lambda i, g: (0, 0))],
        out_specs=pl.BlockSpec((tm, d), lambda i, g: (i, 0)),
        out_shape=jax.ShapeDtypeStruct((t, d), F32),
        scratch_shapes=[pltpu.VMEM((d, tm), F32)],
        compiler_params=_cparams(("parallel", "arbitrary"), vmem=PEER_VMEM_LIMIT),
        name="peer_experts",
    )(ht, u_bf, v_bf.T, cnt, e1, rank2, e2, x2d, g2, final_gain.astype(F32).reshape(1, d))


def _layer(x2d, c, bsz, seq, ada_w, ada_b, norm_mix_gain, norm_ffn_gain, w_in, s5_params, s5_d,
           s5_w_glu, s5_b_glu, lower_bound, hg_norm_gain, pool_w, pool_scale, w_gate, w_branch, w_out,
           peer_w_query, peer_sub_keys, peer_u, peer_v, final_gain, last):
    d = D_MODEL
    mod = _ada(c.astype(F32), ada_w.astype(F32), ada_b.astype(F32))
    sh1, sc1, g1, sh2, sc2, g2 = [m.reshape(bsz, 1, d) for m in jnp.split(mod, 6, axis=-1)]
    wcat = jnp.concatenate([w_gate[n] for n in range(N_BRANCH)] + [w_in], axis=1).astype(BF16)
    proj = _inproj(x2d, norm_mix_gain.astype(F32), sh1, sc1, wcat, seq)
    proj3 = proj.reshape(bsz, seq, CAT_COLS)
    bbd, cbd, tabs = _s5_tables(*s5_params)
    s5o = _s5(proj3, bbd, cbd, tabs, s5_d, s5_w_glu, s5_b_glu)
    hg = _hgrn(proj3, lower_bound, hg_norm_gain.astype(F32))
    sb = _stick_breaking(proj3)
    po = _pool(proj3, pool_w, pool_scale)
    t = bsz * seq
    flat = lambda a: a.reshape(t, BRANCH_WIDTH)
    x1 = _merge(x2d, g1, flat(s5o), flat(hg), flat(sb), flat(po), proj, w_branch, w_out, seq)
    ht, cnt, e1, rank2, e2 = _route(x1, norm_ffn_gain.astype(F32), sh2, sc2,
                                    peer_w_query.T.astype(BF16), peer_sub_keys.astype(BF16), seq)
    return _peer(ht, peer_u.astype(BF16), peer_v.astype(BF16), cnt, e1, rank2, e2, x1, g2, final_gain, last,
                 seq)


def kernel(x, c, ada_w, ada_b, norm_mix_gain, norm_ffn_gain, w_in, s5_lambda_re, s5_lambda_im, s5_log_step, s5_b_re, s5_b_im, s5_c_re, s5_c_im, s5_d, s5_w_glu, s5_b_glu, hg_lb_logits, hg_norm_gain, pool_w, pool_scale, w_gate, w_branch, w_out, peer_w_query, peer_sub_keys, peer_u, peer_v, final_gain):
    bsz, seq, d = x.shape
    depth = ada_w.shape[0]
    lb_soft = jax.nn.softmax(hg_lb_logits.astype(F32), axis=0)
    lower_bounds = jnp.cumsum(lb_soft, axis=0) - lb_soft[0:1]
    x2d = x.astype(F32).reshape(bsz * seq, d)
    for l in range(depth):
        s5_params = (s5_lambda_re[l], s5_lambda_im[l], s5_log_step[l], s5_b_re[l], s5_b_im[l],
                     s5_c_re[l], s5_c_im[l])
        x2d = _layer(x2d, c, bsz, seq, ada_w[l], ada_b[l], norm_mix_gain[l], norm_ffn_gain[l], w_in[l],
                     s5_params, s5_d[l], s5_w_glu[l], s5_b_glu[l], lower_bounds[l], hg_norm_gain[l],
                     pool_w[l], pool_scale[l], w_gate[l], w_branch[l], w_out[l], peer_w_query[l],
                     peer_sub_keys[l], peer_u[l], peer_v[l], final_gain, l == depth - 1)
    return x2d.reshape(bsz, seq, d).astype(x.dtype)
```
